```python
import jax, jax.numpy as jnp
from jax import lax
import numpy as np


D_MODEL = 4096
BATCH = 2
SEQ = 8192
DEPTH = 1

CHUNK = 64
N_META = 16
LEAD_PAD = CHUNK - N_META
EPS = 1e-6

D_MIX = D_MODEL
GLA_WIDTH = D_MIX // 2
GLA_HEADS = 8
GLA_DV = GLA_WIDTH // GLA_HEADS
GLA_DK = GLA_DV // 2
GLA_KEY_WIDTH = GLA_HEADS * GLA_DK
GLA_GATE_RANK = 16
GLA_TAU = 16.0
CONV_WIDTH = D_MIX - GLA_WIDTH
CONV_KERNEL = 31

Q_END = GLA_KEY_WIDTH
K_END = Q_END + GLA_KEY_WIDTH
V_END = K_END + GLA_WIDTH
R_END = V_END + GLA_WIDTH
GD_END = R_END + GLA_GATE_RANK
CA_END = GD_END + CONV_WIDTH
IN_COLS = CA_END + CONV_WIDTH
IN_SPLITS = (Q_END, K_END, V_END, R_END, GD_END, CA_END)

PEER_HEADS = 8
PEER_NKEYS = 128
PEER_NEXPERTS = PEER_NKEYS * PEER_NKEYS
PEER_DKEY = 256
PEER_HALF = PEER_DKEY // 2
PEER_TOPK = 16
PEER_TOKEN_BLOCK = 64

kernel_name = 'hymba_gla_conformer_peer_block'


def rms_norm(x, g):
    xf = x.astype(jnp.float32)
    y = xf * lax.rsqrt(jnp.mean(xf * xf, axis=-1, keepdims=True) + EPS)
    return (y * g.astype(jnp.float32)).astype(x.dtype)


def layer_norm(x, g, b):
    xf = x.astype(jnp.float32)
    mu = jnp.mean(xf, axis=-1, keepdims=True)
    xc = xf - mu
    y = xc * lax.rsqrt(jnp.mean(xc * xc, axis=-1, keepdims=True) + EPS)
    return (y * g.astype(jnp.float32) + b.astype(jnp.float32)).astype(x.dtype)


def gla_chunk_causal(q, k, v, log_a):
    B, L, H, DK = q.shape
    DV = v.shape[-1]
    pad = ((0, 0), (LEAD_PAD, 0), (0, 0), (0, 0))
    q, k, v, log_a = [jnp.pad(t, pad) for t in (q, k, v, log_a)]
    NC = (L + LEAD_PAD) // CHUNK

    def to_chunks(t):
        return t.reshape(B, NC, CHUNK, H, t.shape[-1]).transpose(1, 0, 3, 2, 4)

    qc, kc, vc, ac = [to_chunks(t) for t in (q, k, v, log_a)]
    bcum = jnp.cumsum(ac, axis=3)
    total = bcum[:, :, :, -1:, :]
    k_dec = kc * jnp.exp(total - bcum)
    chunk_decay = jnp.exp(total[:, :, :, 0, :])

    def step(S, xs):
        q_t, k_t, v_t, d_t = xs
        S = d_t[..., None] * S + jnp.einsum('bhck,bhcv->bhkv', k_t, v_t)
        return S, jnp.einsum('bhck,bhkv->bhcv', q_t, S)

    S0 = jnp.zeros((B, H, DK, DV), jnp.float32)
    _, o = lax.scan(step, S0, (qc, k_dec, vc, chunk_decay))
    o = o.transpose(1, 0, 3, 2, 4).reshape(B, NC * CHUNK, H, DV)
    return o[:, LEAD_PAD:]


def token_mix(xn, w_in, w_gate_up, b_gate, gla_norm_g, w_dw, b_dw, conv_ln_g, conv_ln_b, w_out):
    B, L, _ = xn.shape
    p = xn @ w_in
    q, k, v, r, gd, ca, cb = jnp.split(p, IN_SPLITS, axis=-1)

    f32 = jnp.float32
    qh = q.astype(f32).reshape(B, L, GLA_HEADS, GLA_DK) * (GLA_DK ** -0.5)
    kh = k.astype(f32).reshape(B, L, GLA_HEADS, GLA_DK)
    vh = v.astype(f32).reshape(B, L, GLA_HEADS, GLA_DV)
    z = (gd @ w_gate_up + b_gate).astype(f32)
    log_a = (jax.nn.log_sigmoid(z) / GLA_TAU).reshape(B, L, GLA_HEADS, GLA_DK)
    o = gla_chunk_causal(qh, kh, vh, log_a).astype(xn.dtype)
    o = rms_norm(o, gla_norm_g.reshape(GLA_HEADS, GLA_DV)).reshape(B, L, GLA_WIDTH)
    gla_out = o * jax.nn.silu(r)

    u = ca * jax.nn.sigmoid(cb)
    y = lax.conv_general_dilated(
        u, w_dw[:, None, :], window_strides=(1,), padding=[(CONV_KERNEL - 1, 0)],
        dimension_numbers=('NWC', 'WIO', 'NWC'), feature_group_count=CONV_WIDTH) + b_dw
    conv_out = jax.nn.silu(layer_norm(y, conv_ln_g, conv_ln_b))

    return jnp.concatenate([gla_out, conv_out], axis=-1) @ w_out


def peer(xn, w_q, keys1, keys2, u_tab, v_tab):
    B, L, D = xn.shape
    q = (xn @ w_q).reshape(B, L, PEER_HEADS, 2, PEER_HALF)
    s1 = jnp.einsum('blhd,hnd->blhn', q[..., 0, :], keys1).astype(jnp.float32)
    s2 = jnp.einsum('blhd,hnd->blhn', q[..., 1, :], keys2).astype(jnp.float32)
    v1, i1 = lax.top_k(s1, PEER_TOPK)
    v2, i2 = lax.top_k(s2, PEER_TOPK)
    n_cand = PEER_TOPK * PEER_TOPK
    cand = (v1[..., :, None] + v2[..., None, :]).reshape(B, L, PEER_HEADS, n_cand)
    cand_idx = (i1[..., :, None] * PEER_NKEYS + i2[..., None, :]).reshape(B, L, PEER_HEADS, n_cand)
    top_s, pos = lax.top_k(cand, PEER_TOPK)
    idx = jnp.take_along_axis(cand_idx, pos, axis=-1)
    g = jax.nn.softmax(top_s, axis=-1).astype(xn.dtype)

    T = B * L
    HK = PEER_HEADS * PEER_TOPK
    padn = (-T) % PEER_TOKEN_BLOCK
    xt = jnp.pad(xn.reshape(T, D), ((0, padn), (0, 0)))
    it = jnp.pad(idx.reshape(T, HK), ((0, padn), (0, 0)))
    gt = jnp.pad(g.reshape(T, HK), ((0, padn), (0, 0)))
    nb = (T + padn) // PEER_TOKEN_BLOCK

    def block(args):
        xb, ib, gb = args
        a = jnp.einsum('td,ted->te', xb, u_tab[ib])
        hgate = jax.nn.gelu(a, approximate=False) * gb
        return jnp.einsum('te,ted->td', hgate, v_tab[ib])

    out = lax.map(block, (xt.reshape(nb, PEER_TOKEN_BLOCK, D),
                          it.reshape(nb, PEER_TOKEN_BLOCK, HK),
                          gt.reshape(nb, PEER_TOKEN_BLOCK, HK)))
    return out.reshape(-1, D)[:T].reshape(B, L, D)


def setup_inputs(seed: int = 0) -> dict:
    key = jax.random.key(seed)
    ks = jax.random.split(key, 20)
    f32 = jnp.float32
    n = lambda k, s, sc: jax.random.normal(k, s, f32) * sc
    return {
        'x': n(ks[0], (BATCH, SEQ, D_MODEL), 1.0),
        'meta_tokens': n(ks[1], (N_META, D_MODEL), 1.0),
        'norm1_g': 1.0 + n(ks[2], (DEPTH, D_MODEL), 0.02),
        'w_in': n(ks[3], (DEPTH, D_MODEL, IN_COLS), D_MODEL ** -0.5),
        'w_gate_up': n(ks[4], (DEPTH, GLA_GATE_RANK, GLA_KEY_WIDTH), GLA_GATE_RANK ** -0.5),
        'b_gate': n(ks[5], (DEPTH, GLA_KEY_WIDTH), 0.1),
        'gla_norm_g': 1.0 + n(ks[6], (DEPTH, GLA_WIDTH), 0.02),
        'w_dw': n(ks[7], (DEPTH, CONV_KERNEL, CONV_WIDTH), CONV_KERNEL ** -0.5),
        'b_dw': n(ks[8], (DEPTH, CONV_WIDTH), 0.02),
        'conv_ln_g': 1.0 + n(ks[9], (DEPTH, CONV_WIDTH), 0.02),
        'conv_ln_b': n(ks[10], (DEPTH, CONV_WIDTH), 0.02),
        'w_out': n(ks[11], (DEPTH, D_MIX, D_MODEL), D_MIX ** -0.5),
        'norm2_g': 1.0 + n(ks[12], (DEPTH, D_MODEL), 0.02),
        'peer_wq': n(ks[13], (DEPTH, D_MODEL, PEER_HEADS * PEER_DKEY), D_MODEL ** -0.5),
        'peer_keys1': n(ks[14], (DEPTH, PEER_HEADS, PEER_NKEYS, PEER_HALF), PEER_HALF ** -0.5),
        'peer_keys2': n(ks[15], (DEPTH, PEER_HEADS, PEER_NKEYS, PEER_HALF), PEER_HALF ** -0.5),
        'peer_u': n(ks[16], (DEPTH, PEER_NEXPERTS, D_MODEL), D_MODEL ** -0.5),
        'peer_v': n(ks[17], (DEPTH, PEER_NEXPERTS, D_MODEL), PEER_HEADS ** -0.5),
        'final_norm_g': 1.0 + n(ks[18], (D_MODEL,), 0.02),
    }


def reference(x, meta_tokens, norm1_g, w_in, w_gate_up, b_gate, gla_norm_g, w_dw, b_dw,
              conv_ln_g, conv_ln_b, w_out, norm2_g, peer_wq, peer_keys1, peer_keys2,
              peer_u, peer_v, final_norm_g):
    B = x.shape[0]
    meta = jnp.broadcast_to(meta_tokens[None].astype(x.dtype), (B, N_META, D_MODEL))
    h = jnp.concatenate([meta, x], axis=1)
    for l in range(DEPTH):
        h = h + token_mix(rms_norm(h, norm1_g[l]), w_in[l], w_gate_up[l], b_gate[l],
                          gla_norm_g[l], w_dw[l], b_dw[l], conv_ln_g[l], conv_ln_b[l], w_out[l])
        h = h + peer(rms_norm(h, norm2_g[l]), peer_wq[l], peer_keys1[l], peer_keys2[l],
                     peer_u[l], peer_v[l])
    return rms_norm(h[:, N_META:], final_norm_g)
```

```python
import functools

import jax
import jax.numpy as jnp
from jax import lax
from jax.experimental import pallas as pl
from jax.experimental.pallas import tpu as pltpu

F32 = jnp.float32
BF16 = jnp.bfloat16

EPS = 1e-6
CHUNK = 64
GLA_HEADS = 8
GLA_DK = 128
GLA_DV = 256
GLA_KEY_WIDTH = GLA_HEADS * GLA_DK
GLA_WIDTH = GLA_HEADS * GLA_DV
GLA_GATE_RANK = 16
GLA_TAU = 16.0
CONV_WIDTH = 2048
CONV_KERNEL = 31
CONV_HALO = 32
PEER_HEADS = 8
PEER_NKEYS = 128
PEER_HALF = 128
PEER_TOPK = 16
PEER_DCHUNK = 1024

LANES = 128
VMEM_LIMIT = 56 * 1024 * 1024

NT_DIMS = (((1,), (1,)), ((), ()))
TN_DIMS = (((0,), (0,)), ((), ()))


def _sigmoid(x):
    return 1.0 / (1.0 + jnp.exp(-x))


def _rms_norm(xf, g):
    ms = jnp.mean(xf * xf, axis=-1, keepdims=True)
    return xf * lax.rsqrt(ms + EPS) * g


def _params(sem):
    return pltpu.CompilerParams(dimension_semantics=sem, vmem_limit_bytes=VMEM_LIMIT)


def _in_proj_kernel(x_ref, g_ref, w_ref, wgd_ref, p_ref, gd_ref, xn_ref):
    @pl.when(pl.program_id(1) == 0)
    def _():
        xn = _rms_norm(x_ref[...], g_ref[...]).astype(BF16)
        xn_ref[...] = xn
        gd_ref[...] = jnp.dot(xn, wgd_ref[...], preferred_element_type=F32).astype(BF16)

    p_ref[...] = jnp.dot(xn_ref[...], w_ref[...], preferred_element_type=F32).astype(BF16)


def _in_proj(x2, g, w_main, w_gd, tm, tn):
    rows, d = x2.shape
    n = w_main.shape[1]
    return pl.pallas_call(
        _in_proj_kernel,
        grid=(rows // tm, n // tn),
        in_specs=[
            pl.BlockSpec((tm, d), lambda i, j: (i, 0)),
            pl.BlockSpec((1, d), lambda i, j: (0, 0)),
            pl.BlockSpec((d, tn), lambda i, j: (0, j)),
            pl.BlockSpec((d, LANES), lambda i, j: (0, 0)),
        ],
        out_specs=[
            pl.BlockSpec((tm, tn), lambda i, j: (i, j)),
            pl.BlockSpec((tm, LANES), lambda i, j: (i, 0)),
        ],
        out_shape=[
            jax.ShapeDtypeStruct((rows, n), BF16),
            jax.ShapeDtypeStruct((rows, LANES), BF16),
        ],
        scratch_shapes=[pltpu.VMEM((tm, d), BF16)],
        compiler_params=_params(("parallel", "arbitrary")),
        name="in_proj",
    )(x2, g, w_main, w_gd)


def _gla_state_update(k, v, gd, wgu, bg, st_ref):
    c = k.shape[0]
    z = jnp.dot(gd, wgu, preferred_element_type=F32) + bg
    la = (jnp.minimum(z, 0.0) - jnp.log1p(jnp.exp(-jnp.abs(z)))) * (1.0 / GLA_TAU)
    row = lax.broadcasted_iota(jnp.int32, (c, c), 0)
    col = lax.broadcasted_iota(jnp.int32, (c, c), 1)
    tri = jnp.where(col > row, 1.0, 0.0).astype(BF16)
    la_hi = la.astype(BF16)
    la_lo = (la - la_hi.astype(F32)).astype(BF16)
    rev = (jnp.dot(tri, la_hi, preferred_element_type=F32)
           + jnp.dot(tri, la_lo, preferred_element_type=F32))
    total = jnp.sum(la, axis=0, keepdims=True)
    kdec = (k.astype(F32) * jnp.exp(rev)).astype(BF16)
    kv = lax.dot_general(v, kdec, TN_DIMS, preferred_element_type=F32)
    st_ref[...] = st_ref[...] * jnp.exp(total) + kv


def _gla_kernel(q_ref, k_ref, v_ref, r_ref, gd_ref, km_ref, vm_ref, gdm_ref, wgu_ref, bg_ref, gn_ref,
                o_ref, st_ref, *, n_chunks):
    wgu = wgu_ref[...]
    bg = bg_ref[...]
    gn = gn_ref[...]

    @pl.when(pl.program_id(2) == 0)
    def _():
        st_ref[...] = jnp.zeros_like(st_ref)
        _gla_state_update(km_ref[...], vm_ref[...], gdm_ref[...], wgu, bg, st_ref)

    for c in range(n_chunks):
        sl = pl.ds(c * CHUNK, CHUNK)
        _gla_state_update(k_ref[sl, :], v_ref[sl, :], gd_ref[sl, :], wgu, bg, st_ref)
        o = lax.dot_general(q_ref[sl, :], st_ref[...].astype(BF16), NT_DIMS,
                            preferred_element_type=F32) * (GLA_DK ** -0.5)
        on = _rms_norm(o, gn)
        r = r_ref[sl, :].astype(F32)
        o_ref[sl, :] = (on * (r * _sigmoid(r))).astype(BF16)


def _gla(p, gd, pm, gdm, wgu, bg, gn, batch, rb):
    rows = p.shape[0]
    nrb = rows // batch // rb
    kb = GLA_KEY_WIDTH // GLA_DK
    vb = 2 * GLA_KEY_WIDTH // GLA_DV
    rblk = (2 * GLA_KEY_WIDTH + GLA_WIDTH) // GLA_DV
    row = lambda b, h, c: b * nrb + c
    return pl.pallas_call(
        functools.partial(_gla_kernel, n_chunks=rb // CHUNK),
        grid=(batch, GLA_HEADS, nrb),
        in_specs=[
            pl.BlockSpec((rb, GLA_DK), lambda b, h, c: (row(b, h, c), h)),
            pl.BlockSpec((rb, GLA_DK), lambda b, h, c: (row(b, h, c), kb + h)),
            pl.BlockSpec((rb, GLA_DV), lambda b, h, c: (row(b, h, c), vb + h)),
            pl.BlockSpec((rb, GLA_DV), lambda b, h, c: (row(b, h, c), rblk + h)),
            pl.BlockSpec((rb, LANES), lambda b, h, c: (row(b, h, c), 0)),
            pl.BlockSpec((pm.shape[0], GLA_DK), lambda b, h, c: (0, kb + h)),
            pl.BlockSpec((pm.shape[0], GLA_DV), lambda b, h, c: (0, vb + h)),
            pl.BlockSpec((pm.shape[0], LANES), lambda b, h, c: (0, 0)),
            pl.BlockSpec((LANES, GLA_DK), lambda b, h, c: (0, h)),
            pl.BlockSpec((1, GLA_DK), lambda b, h, c: (0, h)),
            pl.BlockSpec((1, GLA_DV), lambda b, h, c: (0, h)),
        ],
        out_specs=pl.BlockSpec((rb, GLA_DV), lambda b, h, c: (row(b, h, c), h)),
        out_shape=jax.ShapeDtypeStruct((rows, GLA_WIDTH), BF16),
        scratch_shapes=[pltpu.VMEM((GLA_DV, GLA_DK), F32)],
        compiler_params=_params(("parallel", "parallel", "arbitrary")),
        name="gla",
    )(p, p, p, p, gd, pm, pm, gdm, wgu, bg, gn)


def _conv_kernel(ca_ref, cb_ref, cam_ref, cbm_ref, w_ref, b_ref, lg_ref, lb_ref, o_ref, u_ref, y_ref, *, tr):
    t = pl.program_id(1)
    n_meta = cam_ref.shape[0]

    @pl.when(t == 0)
    def _():
        um = cam_ref[...].astype(F32) * _sigmoid(cbm_ref[...].astype(F32))
        u_ref[0:CONV_HALO - n_meta, :] = jnp.zeros((CONV_HALO - n_meta, CONV_WIDTH), F32)
        u_ref[CONV_HALO - n_meta:CONV_HALO, :] = um

    @pl.when(t > 0)
    def _():
        u_ref[0:CONV_HALO, :] = u_ref[tr:tr + CONV_HALO, :]

    u_ref[CONV_HALO:CONV_HALO + tr, :] = ca_ref[...].astype(F32) * _sigmoid(cb_ref[...].astype(F32))

    def strip(s, carry):
        cs = pl.ds(pl.multiple_of(s * LANES, LANES), LANES)
        acc = jnp.zeros((tr, LANES), F32)
        for j in range(CONV_KERNEL):
            off = CONV_HALO - (CONV_KERNEL - 1) + j
            acc = acc + w_ref[j:j + 1, cs] * u_ref[off:off + tr, cs]
        y_ref[:, cs] = acc + b_ref[:, cs]
        return carry

    lax.fori_loop(0, CONV_WIDTH // LANES, strip, 0)

    y = y_ref[...]
    mu = jnp.mean(y, axis=-1, keepdims=True)
    yc = y - mu
    var = jnp.mean(yc * yc, axis=-1, keepdims=True)
    yn = yc * lax.rsqrt(var + EPS) * lg_ref[...] + lb_ref[...]
    o_ref[...] = (yn * _sigmoid(yn)).astype(BF16)


def _conv(p, pm, w_dw, b_dw, ln_g, ln_b, batch, tr):
    rows = p.shape[0]
    nt = rows // batch // tr
    cab = (2 * GLA_KEY_WIDTH + 2 * GLA_WIDTH) // CONV_WIDTH
    return pl.pallas_call(
        functools.partial(_conv_kernel, tr=tr),
        grid=(batch, nt),
        in_specs=[
            pl.BlockSpec((tr, CONV_WIDTH), lambda b, t: (b * nt + t, cab)),
            pl.BlockSpec((tr, CONV_WIDTH), lambda b, t: (b * nt + t, cab + 1)),
            pl.BlockSpec((pm.shape[0], CONV_WIDTH), lambda b, t: (0, cab)),
            pl.BlockSpec((pm.shape[0], CONV_WIDTH), lambda b, t: (0, cab + 1)),
            pl.BlockSpec((CONV_HALO, CONV_WIDTH), lambda b, t: (0, 0)),
            pl.BlockSpec((1, CONV_WIDTH), lambda b, t: (0, 0)),
            pl.BlockSpec((1, CONV_WIDTH), lambda b, t: (0, 0)),
            pl.BlockSpec((1, CONV_WIDTH), lambda b, t: (0, 0)),
        ],
        out_specs=pl.BlockSpec((tr, CONV_WIDTH), lambda b, t: (b * nt + t, 0)),
        out_shape=jax.ShapeDtypeStruct((rows, CONV_WIDTH), BF16),
        scratch_shapes=[pltpu.VMEM((CONV_HALO + tr, CONV_WIDTH), F32), pltpu.VMEM((tr, CONV_WIDTH), F32)],
        compiler_params=_params(("parallel", "arbitrary")),
        name="conv",
    )(p, p, pm, pm, w_dw, b_dw, ln_g, ln_b)


def _out_proj_kernel(g_ref, c_ref, wt_ref, wb_ref, x_ref, o_ref):
    acc = jnp.dot(g_ref[...], wt_ref[...], preferred_element_type=F32)
    acc = acc + jnp.dot(c_ref[...], wb_ref[...], preferred_element_type=F32)
    o_ref[...] = x_ref[...] + acc


def _out_proj(gla_out, conv_out, w_out, x2, tm, tn):
    rows, d = x2.shape
    return pl.pallas_call(
        _out_proj_kernel,
        grid=(rows // tm, d // tn),
        in_specs=[
            pl.BlockSpec((tm, GLA_WIDTH), lambda i, j: (i, 0)),
            pl.BlockSpec((tm, CONV_WIDTH), lambda i, j: (i, 0)),
            pl.BlockSpec((GLA_WIDTH, tn), lambda i, j: (0, j)),
            pl.BlockSpec((CONV_WIDTH, tn), lambda i, j: (GLA_WIDTH // CONV_WIDTH, j)),
            pl.BlockSpec((tm, tn), lambda i, j: (i, j)),
        ],
        out_specs=pl.BlockSpec((tm, tn), lambda i, j: (i, j)),
        out_shape=jax.ShapeDtypeStruct((rows, d), F32),
        compiler_params=_params(("parallel", "arbitrary")),
        name="out_proj",
    )(gla_out, conv_out, w_out, w_out, x2)


def _top_desc(s, n):
    vals = []
    for _ in range(n):
        m = jnp.max(s, axis=0, keepdims=True)
        vals.append(m)
        s = jnp.where(s == m, -jnp.inf, s)
    return jnp.concatenate(vals, axis=0)


def _peer_score_kernel(h_ref, g_ref, wq_ref, k1_ref, k2_ref, xn_ref, e1_ref, e2_ref, thr_ref):
    @pl.when(pl.program_id(1) == 0)
    def _():
        xn_ref[...] = _rms_norm(h_ref[...], g_ref[...]).astype(BF16)

    q = jnp.dot(xn_ref[...], wq_ref[...], preferred_element_type=F32).astype(BF16)
    s1 = lax.dot_general(k1_ref[0], q[:, :PEER_HALF], NT_DIMS, preferred_element_type=F32)
    s2 = lax.dot_general(k2_ref[0], q[:, PEER_HALF:], NT_DIMS, preferred_element_type=F32)
    n = PEER_TOPK + 1
    v1 = _top_desc(s1, n)
    v2 = _top_desc(s2, n)
    e1 = jnp.exp(s1 - v1[0:1])
    e2 = jnp.exp(s2 - v2[0:1])
    p1 = jnp.exp(v1 - v1[0:1])
    p2 = jnp.exp(v2 - v2[0:1])
    tm = s1.shape[1]
    p2pad = jnp.concatenate([p2, jnp.full((24 - n, tm), -1.0, F32)], axis=0)
    cands = []
    for a in range(n):
        nb = n // (a + 1)
        rows_b = 24 if nb > 16 else (16 if nb > 8 else 8)
        blk = p1[a:a + 1] * p2pad[0:rows_b]
        rid = lax.broadcasted_iota(jnp.int32, (rows_b, tm), 0)
        cands.append(jnp.where(rid < nb, blk, -1.0))
    c = jnp.concatenate(cands, axis=0)
    z = jnp.zeros((1, tm), F32)
    w_last = None
    for it in range(n):
        m = jnp.max(c, axis=0, keepdims=True)
        if it < PEER_TOPK:
            z = z + m
            w_last = m
        else:
            w_next = m
        c = jnp.where(c == m, -1.0, c)
    rz = 1.0 / z
    e1_ref[0] = e1 * rz
    e2_ref[0] = e2
    thr_ref[0] = 0.5 * (w_last + w_next) * rz


def _peer_score(h1, g2, wq, k1, k2, tm):
    rows, d = h1.shape
    qw = 2 * PEER_HALF
    return pl.pallas_call(
        _peer_score_kernel,
        grid=(rows // tm, PEER_HEADS),
        in_specs=[
            pl.BlockSpec((tm, d), lambda i, h: (i, 0)),
            pl.BlockSpec((1, d), lambda i, h: (0, 0)),
            pl.BlockSpec((d, qw), lambda i, h: (0, h)),
            pl.BlockSpec((1, PEER_NKEYS, PEER_HALF), lambda i, h: (h, 0, 0)),
            pl.BlockSpec((1, PEER_NKEYS, PEER_HALF), lambda i, h: (h, 0, 0)),
        ],
        out_specs=[
            pl.BlockSpec((tm, d), lambda i, h: (i, 0)),
            pl.BlockSpec((1, PEER_NKEYS, tm), lambda i, h: (h, 0, i)),
            pl.BlockSpec((1, PEER_NKEYS, tm), lambda i, h: (h, 0, i)),
            pl.BlockSpec((1, 1, tm), lambda i, h: (h, 0, i)),
        ],
        out_shape=[
            jax.ShapeDtypeStruct((rows, d), BF16),
            jax.ShapeDtypeStruct((PEER_HEADS, PEER_NKEYS, rows), F32),
            jax.ShapeDtypeStruct((PEER_HEADS, PEER_NKEYS, rows), F32),
            jax.ShapeDtypeStruct((PEER_HEADS, 1, rows), F32),
        ],
        compiler_params=_params(("parallel", "arbitrary")),
        name="peer_score",
    )(h1, g2, wq, k1, k2)


def _peer_dense_kernel(xn_ref, u_ref, v_ref, e1_ref, e2_ref, thr_ref, h_ref, gf_ref, o_ref, *, te):
    e = pl.program_id(1)
    n_sub = te // PEER_NKEYS

    @pl.when(e == 0)
    def _():
        o_ref[...] = jnp.zeros_like(o_ref)

    at = lax.dot_general(u_ref[...], xn_ref[...], NT_DIMS, preferred_element_type=F32)
    hg = []
    for c in range(n_sub):
        i1 = e * n_sub + c
        a = at[c * PEER_NKEYS:(c + 1) * PEER_NKEYS]
        g = jnp.zeros_like(a)
        for h in range(PEER_HEADS):
            w = e1_ref[h, pl.ds(i1, 1), :] * e2_ref[h]
            g = g + jnp.where(w >= thr_ref[h], w, 0.0)
        gelu = 0.5 * a * (1.0 + lax.erf(a * (2.0 ** -0.5)))
        hg.append((gelu * g).astype(BF16))
    hgt = jnp.concatenate(hg, axis=0)
    d = o_ref.shape[1]
    for n0 in range(0, d, PEER_DCHUNK):
        cs = slice(n0, n0 + PEER_DCHUNK)
        o_ref[:, cs] += lax.dot_general(hgt, v_ref[:, cs], TN_DIMS, preferred_element_type=F32)

    @pl.when(e == pl.num_programs(1) - 1)
    def _():
        o_ref[...] = _rms_norm(h_ref[...] + o_ref[...], gf_ref[...])


def _peer_dense(xn2, u, v, e1, e2, thr, h1, gf, tm, te):
    rows, d = h1.shape
    n_exp = u.shape[0]
    return pl.pallas_call(
        functools.partial(_peer_dense_kernel, te=te),
        grid=(rows // tm, n_exp // te),
        in_specs=[
            pl.BlockSpec((tm, d), lambda i, e: (i, 0), pipeline_mode=pl.Buffered(1)),
            pl.BlockSpec((te, d), lambda i, e: (e, 0)),
            pl.BlockSpec((te, d), lambda i, e: (e, 0)),
            pl.BlockSpec((PEER_HEADS, PEER_NKEYS, tm), lambda i, e: (0, 0, i), pipeline_mode=pl.Buffered(1)),
            pl.BlockSpec((PEER_HEADS, PEER_NKEYS, tm), lambda i, e: (0, 0, i), pipeline_mode=pl.Buffered(1)),
            pl.BlockSpec((PEER_HEADS, 1, tm), lambda i, e: (0, 0, i)),
            pl.BlockSpec((tm, d), lambda i, e: (i, 0), pipeline_mode=pl.Buffered(1)),
            pl.BlockSpec((1, d), lambda i, e: (0, 0)),
        ],
        out_specs=pl.BlockSpec((tm, d), lambda i, e: (i, 0), pipeline_mode=pl.Buffered(1)),
        out_shape=jax.ShapeDtypeStruct((rows, d), F32),
        compiler_params=_params(("parallel", "arbitrary")),
        name="peer_dense",
    )(xn2, u, v, e1, e2, thr, h1, gf)


def _tile(n, pref):
    t = min(n, pref)
    assert n % t == 0, (n, t)
    return t


def kernel(x, meta_tokens, norm1_g, w_in, w_gate_up, b_gate, gla_norm_g, w_dw, b_dw, conv_ln_g, conv_ln_b,
           w_out, norm2_g, peer_wq, peer_keys1, peer_keys2, peer_u, peer_v, final_norm_g):
    batch, seq, d = x.shape
    rows = batch * seq
    assert norm1_g.shape[0] == 1, "single-layer block"
    assert seq % CHUNK == 0 and meta_tokens.shape[0] <= CONV_HALO and (CHUNK - meta_tokens.shape[0]) >= 0
    x2 = x.reshape(rows, d)

    r_end = 2 * GLA_KEY_WIDTH + 2 * GLA_WIDTH
    gd_end = r_end + GLA_GATE_RANK
    w_in0 = w_in[0]
    w_main = jnp.concatenate([w_in0[:, :r_end], w_in0[:, gd_end:]], axis=1).astype(BF16)
    w_gd = jnp.pad(w_in0[:, r_end:gd_end], ((0, 0), (0, LANES - GLA_GATE_RANK))).astype(BF16)
    wgu = jnp.pad(w_gate_up[0], ((0, LANES - GLA_GATE_RANK), (0, 0))).astype(BF16)
    g1 = norm1_g[0][None]

    tm = _tile(rows, 512)
    p, gd = _in_proj(x2, g1, w_main, w_gd, tm, _tile(w_main.shape[1], 1024))
    pm, gdm = _in_proj(meta_tokens.astype(F32), g1, w_main, w_gd, meta_tokens.shape[0],
                       _tile(w_main.shape[1], 1024))

    gla_out = _gla(p, gd, pm, gdm, wgu, b_gate[0][None], gla_norm_g[0][None], batch, _tile(seq, 512))
    w_dw_p = jnp.pad(w_dw[0], ((0, CONV_HALO - CONV_KERNEL), (0, 0)))
    conv_out = _conv(p, pm, w_dw_p, b_dw[0][None], conv_ln_g[0][None], conv_ln_b[0][None], batch,
                     _tile(seq, 128))
    h1 = _out_proj(gla_out, conv_out, w_out[0].astype(BF16), x2, tm, _tile(d, 1024))

    xn2, e1, e2, thr = _peer_score(h1, norm2_g[0][None], peer_wq[0].astype(BF16),
                                   peer_keys1[0].astype(BF16), peer_keys2[0].astype(BF16), tm)
    out = _peer_dense(xn2, peer_u[0].astype(BF16), peer_v[0].astype(BF16), e1, e2, thr, h1,
                      final_norm_g[None], tm, 512)
    return out.reshape(batch, seq, d)
```

```python
import functools

import jax
import jax.numpy as jnp
from jax import lax
from jax.experimental import pallas as pl
from jax.experimental.pallas import tpu as pltpu

F32 = jnp.float32
BF16 = jnp.bfloat16

EPS = 1e-6
CHUNK = 64
GLA_HEADS = 8
GLA_DK = 128
GLA_DV = 256
GLA_KEY_WIDTH = GLA_HEADS * GLA_DK
GLA_WIDTH = GLA_HEADS * GLA_DV
GLA_GATE_RANK = 16
GLA_TAU = 16.0
CONV_WIDTH = 2048
CONV_KERNEL = 31
CONV_HALO = 32
PEER_HEADS = 8
PEER_NKEYS = 128
PEER_HALF = 128
PEER_TOPK = 16
PEER_DPIECE = 512

LANES = 128
SUBLANES = 8
VMEM_LIMIT = 56 * 1024 * 1024

NT_DIMS = (((1,), (1,)), ((), ()))
TN_DIMS = (((0,), (0,)), ((), ()))


def _sigmoid(x):
    return 1.0 / (1.0 + jnp.exp(-x))


def _rms_norm(xf, g):
    ms = jnp.mean(xf * xf, axis=-1, keepdims=True)
    return xf * lax.rsqrt(ms + EPS) * g


def _params(sem, flags=None):
    return pltpu.CompilerParams(dimension_semantics=sem, vmem_limit_bytes=VMEM_LIMIT, flags=flags)


def _in_proj_kernel(x_ref, g_ref, w_ref, wgd_ref, p_ref, gd_ref, xn_ref):
    @pl.when(pl.program_id(1) == 0)
    def _():
        xn = _rms_norm(x_ref[...], g_ref[...]).astype(BF16)
        xn_ref[...] = xn
        gd_ref[...] = jnp.dot(xn, wgd_ref[...], preferred_element_type=F32).astype(BF16)

    p_ref[...] = jnp.dot(xn_ref[...], w_ref[...], preferred_element_type=F32).astype(BF16)


def _in_proj(x2, g, w_main, w_gd, tm, tn):
    rows, d = x2.shape
    n = w_main.shape[1]
    return pl.pallas_call(
        _in_proj_kernel,
        grid=(rows // tm, n // tn),
        in_specs=[
            pl.BlockSpec((tm, d), lambda i, j: (i, 0)),
            pl.BlockSpec((1, d), lambda i, j: (0, 0)),
            pl.BlockSpec((d, tn), lambda i, j: (0, j)),
            pl.BlockSpec((d, LANES), lambda i, j: (0, 0)),
        ],
        out_specs=[
            pl.BlockSpec((tm, tn), lambda i, j: (i, j)),
            pl.BlockSpec((tm, LANES), lambda i, j: (i, 0)),
        ],
        out_shape=[
            jax.ShapeDtypeStruct((rows, n), BF16),
            jax.ShapeDtypeStruct((rows, LANES), BF16),
        ],
        scratch_shapes=[pltpu.VMEM((tm, d), BF16)],
        compiler_params=_params(("parallel", "arbitrary")),
        name="in_proj",
    )(x2, g, w_main, w_gd)


def _gla_decayed_keys(k, gd, tri, wgu, bg):
    z = jnp.dot(gd, wgu, preferred_element_type=F32) + bg
    la = (jnp.minimum(z, 0.0) - jnp.log1p(jnp.exp(-jnp.abs(z)))) * (1.0 / GLA_TAU)
    la_hi = la.astype(BF16)
    la_lo = (la - la_hi.astype(F32)).astype(BF16)
    rev = (jnp.dot(tri, la_hi, preferred_element_type=F32)
           + jnp.dot(tri, la_lo, preferred_element_type=F32))
    return (k.astype(F32) * jnp.exp(rev)).astype(BF16), la


def _gla_head_blocks(v, kdec, n_heads):
    full = lax.dot_general(v, kdec, TN_DIMS, preferred_element_type=F32)
    return [full[h * GLA_DV:(h + 1) * GLA_DV, h * GLA_DK:(h + 1) * GLA_DK] for h in range(n_heads)]


def _gla_kernel(q_ref, k_ref, v_ref, r_ref, gd_ref, km_ref, vm_ref, gdm_ref, wgu_ref, bg_ref, gn_ref, tri_ref,
                o_ref, st_ref, *, n_chunks, n_heads):
    wgu = wgu_ref[...]
    bg = bg_ref[...]
    n_meta = km_ref.shape[0]

    @pl.when(pl.program_id(2) == 0)
    def _():
        kdec_m, _ = _gla_decayed_keys(km_ref[...], gdm_ref[...], tri_ref[0:n_meta, 0:n_meta], wgu, bg)
        for h, blk in enumerate(_gla_head_blocks(vm_ref[...], kdec_m, n_heads)):
            st_ref[h] = blk

    kdec, la = _gla_decayed_keys(k_ref[...], gd_ref[...], tri_ref[...], wgu, bg)
    sts = [st_ref[h] for h in range(n_heads)]
    zero = jnp.zeros((GLA_DV, GLA_DK), BF16)
    for c in range(n_chunks):
        sl = slice(c * CHUNK, (c + 1) * CHUNK)
        decay = jnp.exp(jnp.sum(la[sl], axis=0, keepdims=True))
        kvs = _gla_head_blocks(v_ref[sl, :], kdec[sl], n_heads)
        rows = []
        for h in range(n_heads):
            sts[h] = sts[h] * decay[:, h * GLA_DK:(h + 1) * GLA_DK] + kvs[h]
            rows.append(jnp.concatenate([zero] * h + [sts[h].astype(BF16)] + [zero] * (n_heads - 1 - h), axis=1))
        s_blockdiag = jnp.concatenate(rows, axis=0)
        o = lax.dot_general(q_ref[sl, :], s_blockdiag, NT_DIMS,
                            preferred_element_type=F32) * (GLA_DK ** -0.5)
        on = jnp.concatenate([_rms_norm(o[:, h * GLA_DV:(h + 1) * GLA_DV], gn_ref[:, h * GLA_DV:(h + 1) * GLA_DV])
                              for h in range(n_heads)], axis=1)
        r = r_ref[sl, :].astype(F32)
        o_ref[sl, :] = (on * (r * _sigmoid(r))).astype(BF16)
    for h in range(n_heads):
        st_ref[h] = sts[h]


def _gla(p, gd, pm, gdm, wgu, bg, gn, batch, rb, hps):
    rows = p.shape[0]
    nrb = rows // batch // rb
    kw, vw = hps * GLA_DK, hps * GLA_DV
    kb = GLA_KEY_WIDTH // kw
    vb = 2 * GLA_KEY_WIDTH // vw
    rblk = (2 * GLA_KEY_WIDTH + GLA_WIDTH) // vw
    n_meta = pm.shape[0]
    frame = jnp.arange(rb)
    tri = ((frame[:, None] // CHUNK == frame[None, :] // CHUNK) & (frame[None, :] > frame[:, None])).astype(BF16)
    row = lambda b, h, c: b * nrb + c
    return pl.pallas_call(
        functools.partial(_gla_kernel, n_chunks=rb // CHUNK, n_heads=hps),
        grid=(batch, GLA_HEADS // hps, nrb),
        in_specs=[
            pl.BlockSpec((rb, kw), lambda b, h, c: (row(b, h, c), h)),
            pl.BlockSpec((rb, kw), lambda b, h, c: (row(b, h, c), kb + h)),
            pl.BlockSpec((rb, vw), lambda b, h, c: (row(b, h, c), vb + h)),
            pl.BlockSpec((rb, vw), lambda b, h, c: (row(b, h, c), rblk + h)),
            pl.BlockSpec((rb, LANES), lambda b, h, c: (row(b, h, c), 0)),
            pl.BlockSpec((n_meta, kw), lambda b, h, c: (0, kb + h)),
            pl.BlockSpec((n_meta, vw), lambda b, h, c: (0, vb + h)),
            pl.BlockSpec((n_meta, LANES), lambda b, h, c: (0, 0)),
            pl.BlockSpec((LANES, kw), lambda b, h, c: (0, h)),
            pl.BlockSpec((1, kw), lambda b, h, c: (0, h)),
            pl.BlockSpec((1, vw), lambda b, h, c: (0, h)),
            pl.BlockSpec((rb, rb), lambda b, h, c: (0, 0)),
        ],
        out_specs=pl.BlockSpec((rb, vw), lambda b, h, c: (row(b, h, c), h)),
        out_shape=jax.ShapeDtypeStruct((rows, GLA_WIDTH), BF16),
        scratch_shapes=[pltpu.VMEM((hps, GLA_DV, GLA_DK), F32)],
        compiler_params=_params(("parallel", "parallel", "arbitrary")),
        name="gla",
    )(p, p, p, p, gd, pm, pm, gdm, wgu, bg, gn, tri)


def _conv_kernel(ca_ref, cb_ref, cam_ref, cbm_ref, w_ref, b_ref, lg_ref, lb_ref, o_ref, u_ref, y_ref, *, tr):
    t = pl.program_id(1)
    n_meta = cam_ref.shape[0]

    @pl.when(t == 0)
    def _():
        um = cam_ref[...].astype(F32) * _sigmoid(cbm_ref[...].astype(F32))
        u_ref[0:CONV_HALO - n_meta, :] = jnp.zeros((CONV_HALO - n_meta, CONV_WIDTH), F32)
        u_ref[CONV_HALO - n_meta:CONV_HALO, :] = um

    @pl.when(t > 0)
    def _():
        u_ref[0:CONV_HALO, :] = u_ref[tr:tr + CONV_HALO, :]

    u_ref[CONV_HALO:CONV_HALO + tr, :] = ca_ref[...].astype(F32) * _sigmoid(cb_ref[...].astype(F32))

    def strip(s, carry):
        cs = pl.ds(pl.multiple_of(s * LANES, LANES), LANES)
        x = u_ref[:, cs]
        n = CONV_HALO + tr
        acc = jnp.zeros((tr, LANES), F32)
        for sh in range(SUBLANES):
            xs = x if sh == 0 else pltpu.roll(x, n - sh, axis=0)
            for j in range(CONV_KERNEL):
                off = CONV_HALO - (CONV_KERNEL - 1) + j
                if off % SUBLANES == sh:
                    acc = acc + w_ref[j:j + 1, cs] * xs[off - sh:off - sh + tr]
        y_ref[:, cs] = acc + b_ref[:, cs]
        return carry

    lax.fori_loop(0, CONV_WIDTH // LANES, strip, 0)

    y = y_ref[...]
    mu = jnp.mean(y, axis=-1, keepdims=True)
    yc = y - mu
    var = jnp.mean(yc * yc, axis=-1, keepdims=True)
    yn = yc * lax.rsqrt(var + EPS) * lg_ref[...] + lb_ref[...]
    o_ref[...] = (yn * _sigmoid(yn)).astype(BF16)


def _conv(p, pm, w_dw, b_dw, ln_g, ln_b, batch, tr):
    rows = p.shape[0]
    nt = rows // batch // tr
    cab = (2 * GLA_KEY_WIDTH + 2 * GLA_WIDTH) // CONV_WIDTH
    return pl.pallas_call(
        functools.partial(_conv_kernel, tr=tr),
        grid=(batch, nt),
        in_specs=[
            pl.BlockSpec((tr, CONV_WIDTH), lambda b, t: (b * nt + t, cab)),
            pl.BlockSpec((tr, CONV_WIDTH), lambda b, t: (b * nt + t, cab + 1)),
            pl.BlockSpec((pm.shape[0], CONV_WIDTH), lambda b, t: (0, cab)),
            pl.BlockSpec((pm.shape[0], CONV_WIDTH), lambda b, t: (0, cab + 1)),
            pl.BlockSpec((CONV_HALO, CONV_WIDTH), lambda b, t: (0, 0)),
            pl.BlockSpec((1, CONV_WIDTH), lambda b, t: (0, 0)),
            pl.BlockSpec((1, CONV_WIDTH), lambda b, t: (0, 0)),
            pl.BlockSpec((1, CONV_WIDTH), lambda b, t: (0, 0)),
        ],
        out_specs=pl.BlockSpec((tr, CONV_WIDTH), lambda b, t: (b * nt + t, 0)),
        out_shape=jax.ShapeDtypeStruct((rows, CONV_WIDTH), BF16),
        scratch_shapes=[pltpu.VMEM((CONV_HALO + tr, CONV_WIDTH), F32), pltpu.VMEM((tr, CONV_WIDTH), F32)],
        compiler_params=_params(("parallel", "arbitrary")),
        name="conv",
    )(p, p, pm, pm, w_dw, b_dw, ln_g, ln_b)


def _out_proj_kernel(g_ref, c_ref, wt_ref, wb_ref, x_ref, o_ref):
    acc = jnp.dot(g_ref[...], wt_ref[...], preferred_element_type=F32)
    acc = acc + jnp.dot(c_ref[...], wb_ref[...], preferred_element_type=F32)
    o_ref[...] = x_ref[...] + acc


def _out_proj(gla_out, conv_out, w_out, x2, tm, tn):
    rows, d = x2.shape
    return pl.pallas_call(
        _out_proj_kernel,
        grid=(rows // tm, d // tn),
        in_specs=[
            pl.BlockSpec((tm, GLA_WIDTH), lambda i, j: (i, 0)),
            pl.BlockSpec((tm, CONV_WIDTH), lambda i, j: (i, 0)),
            pl.BlockSpec((GLA_WIDTH, tn), lambda i, j: (0, j)),
            pl.BlockSpec((CONV_WIDTH, tn), lambda i, j: (GLA_WIDTH // CONV_WIDTH, j)),
            pl.BlockSpec((tm, tn), lambda i, j: (i, j)),
        ],
        out_specs=pl.BlockSpec((tm, tn), lambda i, j: (i, j)),
        out_shape=jax.ShapeDtypeStruct((rows, d), F32),
        compiler_params=_params(("parallel", "arbitrary")),
        name="out_proj",
    )(gla_out, conv_out, w_out, w_out, x2)


def _top_desc(s, n):
    vals = []
    for _ in range(n):
        m = jnp.max(s, axis=0, keepdims=True)
        vals.append(m)
        s = jnp.where(s == m, -jnp.inf, s)
    return jnp.concatenate(vals, axis=0)


def _peer_score_kernel(h_ref, g_ref, wq_ref, k1_ref, k2_ref, xn_ref, e1_ref, e2_ref, thr_ref):
    @pl.when(pl.program_id(1) == 0)
    def _():
        xn_ref[...] = _rms_norm(h_ref[...], g_ref[...]).astype(BF16)

    q = jnp.dot(xn_ref[...], wq_ref[...], preferred_element_type=F32).astype(BF16)
    s1 = lax.dot_general(k1_ref[0], q[:, :PEER_HALF], NT_DIMS, preferred_element_type=F32)
    s2 = lax.dot_general(k2_ref[0], q[:, PEER_HALF:], NT_DIMS, preferred_element_type=F32)
    n = PEER_TOPK + 1
    v1 = _top_desc(s1, n)
    v2 = _top_desc(s2, n)
    e1 = jnp.exp(s1 - v1[0:1])
    e2 = jnp.exp(s2 - v2[0:1])
    p1 = jnp.exp(v1 - v1[0:1])
    p2 = jnp.exp(v2 - v2[0:1])
    tm = s1.shape[1]
    p2pad = jnp.concatenate([p2, jnp.full((24 - n, tm), -1.0, F32)], axis=0)
    cands = []
    for a in range(n):
        nb = n // (a + 1)
        rows_b = 24 if nb > 16 else (16 if nb > 8 else 8)
        blk = p1[a:a + 1] * p2pad[0:rows_b]
        rid = lax.broadcasted_iota(jnp.int32, (rows_b, tm), 0)
        cands.append(jnp.where(rid < nb, blk, -1.0))
    c = jnp.concatenate(cands, axis=0)
    z = jnp.zeros((1, tm), F32)
    w_last = None
    for it in range(n):
        m = jnp.max(c, axis=0, keepdims=True)
        if it < PEER_TOPK:
            z = z + m
            w_last = m
        else:
            w_next = m
        c = jnp.where(c == m, -1.0, c)
    rz = 1.0 / z
    e1n = e1 * rz
    for tb in range(tm // LANES):
        e1_ref[0, tb] = e1n[:, tb * LANES:(tb + 1) * LANES]
    e2_ref[0] = e2
    thr_ref[0] = 0.5 * (w_last + w_next) * rz


def _peer_score(h1, g2, wq, k1, k2, tm):
    rows, d = h1.shape
    qw = 2 * PEER_HALF
    return pl.pallas_call(
        _peer_score_kernel,
        grid=(rows // tm, PEER_HEADS),
        in_specs=[
            pl.BlockSpec((tm, d), lambda i, h: (i, 0)),
            pl.BlockSpec((1, d), lambda i, h: (0, 0)),
            pl.BlockSpec((d, qw), lambda i, h: (0, h)),
            pl.BlockSpec((1, PEER_NKEYS, PEER_HALF), lambda i, h: (h, 0, 0)),
            pl.BlockSpec((1, PEER_NKEYS, PEER_HALF), lambda i, h: (h, 0, 0)),
        ],
        out_specs=[
            pl.BlockSpec((tm, d), lambda i, h: (i, 0)),
            pl.BlockSpec((1, tm // LANES, PEER_NKEYS, LANES), lambda i, h: (h, i, 0, 0)),
            pl.BlockSpec((1, PEER_NKEYS, tm), lambda i, h: (h, 0, i)),
            pl.BlockSpec((1, 1, tm), lambda i, h: (h, 0, i)),
        ],
        out_shape=[
            jax.ShapeDtypeStruct((rows, d), BF16),
            jax.ShapeDtypeStruct((PEER_HEADS, rows // LANES, PEER_NKEYS, LANES), F32),
            jax.ShapeDtypeStruct((PEER_HEADS, PEER_NKEYS, rows), F32),
            jax.ShapeDtypeStruct((PEER_HEADS, 1, rows), F32),
        ],
        compiler_params=_params(("parallel", "arbitrary")),
        name="peer_score",
    )(h1, g2, wq, k1, k2)


def _peer_expert_up(xn_ref, u_ref, at_ref):
    at_ref[...] = lax.dot_general(u_ref[...], xn_ref[...], NT_DIMS, preferred_element_type=F32)


def _after(row, anchor, zero_ref):
    z = pltpu.bitcast(anchor, jnp.int32) & zero_ref[...]
    return row + pltpu.bitcast(z, F32)[0:1, :]


def _peer_expert_gate(i1, c, tb, hg_cur, e1_ref, e2_ref, thr_ref, at_ref, anchor, zero_ref):
    rs = slice(c * PEER_NKEYS, (c + 1) * PEER_NKEYS)
    ts = slice(tb * LANES, (tb + 1) * LANES)
    g = jnp.zeros((PEER_NKEYS, LANES), F32)
    for h in range(PEER_HEADS):
        e1_row = e1_ref[h, tb, pl.ds(i1, 1), :]
        if anchor is not None:
            e1_row = _after(e1_row, anchor, zero_ref)
        w = e1_row * e2_ref[h, :, ts]
        g = g + jnp.where(w >= thr_ref[h, :, ts], w, 0.0)
    a = at_ref[rs, ts]
    gelu = 0.5 * a * (1.0 + lax.erf(a * (2.0 ** -0.5)))
    hg_cur[ts, rs] = (gelu * g).T.astype(BF16)


def _peer_expert_down(hg_prev, v_ref, o_ref, piece):
    cs = slice(piece * PEER_DPIECE, (piece + 1) * PEER_DPIECE)
    r = jnp.dot(hg_prev[...], v_ref[:, cs], preferred_element_type=F32)
    o_ref[:, cs] += r
    return r[r.shape[0] - SUBLANES:, PEER_DPIECE - LANES:]


def _peer_dense_kernel(xn_ref, u_ref, v_ref, e1_ref, e2_ref, thr_ref, h_ref, gf_ref, zero_ref, o_ref,
                       at_ref, hg0_ref, hg1_ref, *, te, n_tiles):
    e = pl.program_id(1)
    n_sub = te // PEER_NKEYS
    tm = at_ref.shape[1]
    n_pieces = o_ref.shape[1] // PEER_DPIECE
    hg = (hg0_ref, hg1_ref)
    tiles = [(c, tb) for c in range(n_sub) for tb in range(tm // LANES)]

    def gate(tile, hg_cur, anchor):
        c, tb = tile
        _peer_expert_gate(e * n_sub + c, c, tb, hg_cur, e1_ref, e2_ref, thr_ref, at_ref, anchor, zero_ref)

    @pl.when(e == 0)
    def _():
        o_ref[...] = jnp.zeros_like(o_ref)
        _peer_expert_up(xn_ref, u_ref, at_ref)
        for tile in tiles:
            gate(tile, hg[0], None)

    def steady(parity):
        _peer_expert_up(xn_ref, u_ref, at_ref)
        per_piece = -(-len(tiles) // n_pieces)
        anchor = None
        for p in range(n_pieces):
            done = _peer_expert_down(hg[1 - parity], v_ref, o_ref, p)
            for tile in tiles[p * per_piece:(p + 1) * per_piece]:
                gate(tile, hg[parity], anchor)
            anchor = done

    for parity in (0, 1):
        pl.when((e > 0) & (e < n_tiles) & (e % 2 == parity))(functools.partial(steady, parity))

    @pl.when(e == n_tiles)
    def _():
        for p in range(n_pieces):
            _peer_expert_down(hg[(n_tiles - 1) % 2], v_ref, o_ref, p)
        o_ref[...] = _rms_norm(h_ref[...] + o_ref[...], gf_ref[...])


def _peer_dense(xn2, u, v, e1, e2, thr, h1, gf, tm, te):
    rows, d = h1.shape
    n_tiles = u.shape[0] // te
    zero = jnp.zeros((SUBLANES, LANES), jnp.int32)
    return pl.pallas_call(
        functools.partial(_peer_dense_kernel, te=te, n_tiles=n_tiles),
        grid=(rows // tm, n_tiles + 1),
        in_specs=[
            pl.BlockSpec((tm, d), lambda i, e: (i, 0), pipeline_mode=pl.Buffered(1)),
            pl.BlockSpec((te, d), lambda i, e: (jnp.minimum(e, n_tiles - 1), 0)),
            pl.BlockSpec((te, d), lambda i, e: (jnp.maximum(e - 1, 0), 0)),
            pl.BlockSpec((PEER_HEADS, tm // LANES, PEER_NKEYS, LANES), lambda i, e: (0, i, 0, 0),
                         pipeline_mode=pl.Buffered(1)),
            pl.BlockSpec((PEER_HEADS, PEER_NKEYS, tm), lambda i, e: (0, 0, i), pipeline_mode=pl.Buffered(1)),
            pl.BlockSpec((PEER_HEADS, 1, tm), lambda i, e: (0, 0, i)),
            pl.BlockSpec((tm, d), lambda i, e: (i, 0), pipeline_mode=pl.Buffered(1)),
            pl.BlockSpec((1, d), lambda i, e: (0, 0)),
            pl.BlockSpec((SUBLANES, LANES), lambda i, e: (0, 0)),
        ],
        out_specs=pl.BlockSpec((tm, d), lambda i, e: (i, 0), pipeline_mode=pl.Buffered(1)),
        out_shape=jax.ShapeDtypeStruct((rows, d), F32),
        scratch_shapes=[pltpu.VMEM((te, tm), F32), pltpu.VMEM((tm, te), BF16), pltpu.VMEM((tm, te), BF16)],
        compiler_params=_params(("parallel", "arbitrary")),
        name="peer_dense",
    )(xn2, u, v, e1, e2, thr, h1, gf, zero)


def _tile(n, pref):
    t = min(n, pref)
    assert n % t == 0, (n, t)
    return t


def kernel(x, meta_tokens, norm1_g, w_in, w_gate_up, b_gate, gla_norm_g, w_dw, b_dw, conv_ln_g, conv_ln_b,
           w_out, norm2_g, peer_wq, peer_keys1, peer_keys2, peer_u, peer_v, final_norm_g):
    batch, seq, d = x.shape
    rows = batch * seq
    assert norm1_g.shape[0] == 1, "single-layer block"
    assert seq % CHUNK == 0 and meta_tokens.shape[0] <= CONV_HALO and (CHUNK - meta_tokens.shape[0]) >= 0
    x2 = x.reshape(rows, d)

    r_end = 2 * GLA_KEY_WIDTH + 2 * GLA_WIDTH
    gd_end = r_end + GLA_GATE_RANK
    w_in0 = w_in[0]
    w_main = jnp.concatenate([w_in0[:, :r_end].astype(BF16), w_in0[:, gd_end:].astype(BF16)], axis=1)
    w_gd = jnp.pad(w_in0[:, r_end:gd_end], ((0, 0), (0, LANES - GLA_GATE_RANK))).astype(BF16)
    wgu = jnp.pad(w_gate_up[0], ((0, LANES - GLA_GATE_RANK), (0, 0))).astype(BF16)
    g1 = norm1_g[0][None]

    tm = _tile(rows, 512)
    p, gd = _in_proj(x2, g1, w_main, w_gd, tm, _tile(w_main.shape[1], 1024))
    pm, gdm = _in_proj(meta_tokens.astype(F32), g1, w_main, w_gd, meta_tokens.shape[0],
                       _tile(w_main.shape[1], 1024))

    gla_out = _gla(p, gd, pm, gdm, wgu, b_gate[0][None], gla_norm_g[0][None], batch, _tile(seq, 512), 2)
    w_dw_p = jnp.pad(w_dw[0], ((0, CONV_HALO - CONV_KERNEL), (0, 0)))
    conv_out = _conv(p, pm, w_dw_p, b_dw[0][None], conv_ln_g[0][None], conv_ln_b[0][None], batch,
                     _tile(seq, 128))
    h1 = _out_proj(gla_out, conv_out, w_out[0].astype(BF16), x2, tm, _tile(d, 1024))

    xn2, e1, e2, thr = _peer_score(h1, norm2_g[0][None], peer_wq[0].astype(BF16),
                                   peer_keys1[0].astype(BF16), peer_keys2[0].astype(BF16), tm)
    out = _peer_dense(xn2, peer_u[0].astype(BF16), peer_v[0].astype(BF16), e1, e2, thr, h1,
                      final_norm_g[None], tm, 512)
    return out.reshape(batch, seq, d)
```

```python
import functools

import jax
import jax.numpy as jnp
from jax import lax
from jax.experimental import pallas as pl
from jax.experimental.pallas import tpu as pltpu

F32 = jnp.float32
BF16 = jnp.bfloat16

EPS = 1e-6
CHUNK = 64
GLA_HEADS = 8
GLA_DK = 128
GLA_DV = 256
GLA_KEY_WIDTH = GLA_HEADS * GLA_DK
GLA_WIDTH = GLA_HEADS * GLA_DV
GLA_GATE_RANK = 16
GLA_TAU = 16.0
CONV_WIDTH = 2048
CONV_KERNEL = 31
CONV_HALO = 32
PEER_HEADS = 8
PEER_NKEYS = 128
PEER_HALF = 128
PEER_TOPK = 16
PEER_UPIECE = 256
PEER_DPIECE = 256

LANES = 128
SUBLANES = 8
VMEM_LIMIT = 56 * 1024 * 1024

NT_DIMS = (((1,), (1,)), ((), ()))
TN_DIMS = (((0,), (0,)), ((), ()))


def _sigmoid(x):
    return 1.0 / (1.0 + jnp.exp(-x))


def _rms_norm(xf, g):
    ms = jnp.mean(xf * xf, axis=-1, keepdims=True)
    return xf * lax.rsqrt(ms + EPS) * g


def _params(sem, flags=None):
    return pltpu.CompilerParams(dimension_semantics=sem, vmem_limit_bytes=VMEM_LIMIT, flags=flags)


def _in_proj_kernel(x_ref, g_ref, w_ref, wgd_ref, p_ref, gd_ref, xn_ref):
    @pl.when(pl.program_id(1) == 0)
    def _():
        xn = _rms_norm(x_ref[...], g_ref[...]).astype(BF16)
        xn_ref[...] = xn
        gd_ref[...] = lax.dot_general(xn, wgd_ref[...], NT_DIMS, preferred_element_type=F32).astype(BF16)

    p_ref[...] = lax.dot_general(xn_ref[...], w_ref[...], NT_DIMS, preferred_element_type=F32).astype(BF16)


def _in_proj(x2, g, w_t, w_gd_t, tm, tn, skip_at, skip_n):
    rows, d = x2.shape
    n = w_t.shape[0] - skip_n
    assert skip_at % tn == 0 and n % tn == 0
    return pl.pallas_call(
        _in_proj_kernel,
        grid=(rows // tm, n // tn),
        in_specs=[
            pl.BlockSpec((tm, d), lambda i, j: (i, 0)),
            pl.BlockSpec((1, d), lambda i, j: (0, 0)),
            pl.BlockSpec((pl.Element(tn), pl.Element(d)),
                         lambda i, j: (pl.multiple_of(j * tn + jnp.where(j * tn >= skip_at, skip_n, 0), skip_n), 0)),
            pl.BlockSpec((LANES, d), lambda i, j: (0, 0)),
        ],
        out_specs=[
            pl.BlockSpec((tm, tn), lambda i, j: (i, j)),
            pl.BlockSpec((tm, LANES), lambda i, j: (i, 0)),
        ],
        out_shape=[
            jax.ShapeDtypeStruct((rows, n), BF16),
            jax.ShapeDtypeStruct((rows, LANES), BF16),
        ],
        scratch_shapes=[pltpu.VMEM((tm, d), BF16)],
        compiler_params=_params(("parallel", "arbitrary")),
        name="in_proj",
    )(x2, g, w_t, w_gd_t)


def _gla_decayed_keys(k, gd, tri, wgu, bg):
    z = jnp.dot(gd, wgu, preferred_element_type=F32) + bg
    la = (jnp.minimum(z, 0.0) - jnp.log1p(jnp.exp(-jnp.abs(z)))) * (1.0 / GLA_TAU)
    la_hi = la.astype(BF16)
    la_lo = (la - la_hi.astype(F32)).astype(BF16)
    rev = (jnp.dot(tri, la_hi, preferred_element_type=F32)
           + jnp.dot(tri, la_lo, preferred_element_type=F32))
    return (k.astype(F32) * jnp.exp(rev)).astype(BF16), la


def _gla_head_blocks(v, kdec, n_heads):
    full = lax.dot_general(v, kdec, TN_DIMS, preferred_element_type=F32)
    return [full[h * GLA_DV:(h + 1) * GLA_DV, h * GLA_DK:(h + 1) * GLA_DK] for h in range(n_heads)]


def _gla_kernel(q_ref, k_ref, v_ref, r_ref, gd_ref, km_ref, vm_ref, gdm_ref, wgu_ref, bg_ref, gn_ref, tri_ref,
                o_ref, st_ref, *, n_chunks, n_heads):
    wgu = wgu_ref[...]
    bg = bg_ref[...]
    n_meta = km_ref.shape[0]

    @pl.when(pl.program_id(2) == 0)
    def _():
        kdec_m, _ = _gla_decayed_keys(km_ref[...], gdm_ref[...], tri_ref[0:n_meta, 0:n_meta], wgu, bg)
        for h, blk in enumerate(_gla_head_blocks(vm_ref[...], kdec_m, n_heads)):
            st_ref[h] = blk

    kdec, la = _gla_decayed_keys(k_ref[...], gd_ref[...], tri_ref[...], wgu, bg)
    sts = [st_ref[h] for h in range(n_heads)]
    zero = jnp.zeros((GLA_DV, GLA_DK), BF16)
    for c in range(n_chunks):
        sl = slice(c * CHUNK, (c + 1) * CHUNK)
        decay = jnp.exp(jnp.sum(la[sl], axis=0, keepdims=True))
        kvs = _gla_head_blocks(v_ref[sl, :], kdec[sl], n_heads)
        rows = []
        for h in range(n_heads):
            sts[h] = sts[h] * decay[:, h * GLA_DK:(h + 1) * GLA_DK] + kvs[h]
            rows.append(jnp.concatenate([zero] * h + [sts[h].astype(BF16)] + [zero] * (n_heads - 1 - h), axis=1))
        s_blockdiag = jnp.concatenate(rows, axis=0)
        o = lax.dot_general(q_ref[sl, :], s_blockdiag, NT_DIMS,
                            preferred_element_type=F32) * (GLA_DK ** -0.5)
        on = jnp.concatenate([_rms_norm(o[:, h * GLA_DV:(h + 1) * GLA_DV], gn_ref[:, h * GLA_DV:(h + 1) * GLA_DV])
                              for h in range(n_heads)], axis=1)
        r = r_ref[sl, :].astype(F32)
        o_ref[sl, :] = (on * (r * _sigmoid(r))).astype(BF16)
    for h in range(n_heads):
        st_ref[h] = sts[h]


def _gla(p, gd, pm, gdm, wgu, bg, gn, batch, rb, hps):
    rows = p.shape[0]
    nrb = rows // batch // rb
    kw, vw = hps * GLA_DK, hps * GLA_DV
    kb = GLA_KEY_WIDTH // kw
    vb = 2 * GLA_KEY_WIDTH // vw
    rblk = (2 * GLA_KEY_WIDTH + GLA_WIDTH) // vw
    n_meta = pm.shape[0]
    frame = jnp.arange(rb)
    tri = ((frame[:, None] // CHUNK == frame[None, :] // CHUNK) & (frame[None, :] > frame[:, None])).astype(BF16)
    row = lambda b, h, c: b * nrb + c
    return pl.pallas_call(
        functools.partial(_gla_kernel, n_chunks=rb // CHUNK, n_heads=hps),
        grid=(batch, GLA_HEADS // hps, nrb),
        in_specs=[
            pl.BlockSpec((rb, kw), lambda b, h, c: (row(b, h, c), h)),
            pl.BlockSpec((rb, kw), lambda b, h, c: (row(b, h, c), kb + h)),
            pl.BlockSpec((rb, vw), lambda b, h, c: (row(b, h, c), vb + h)),
            pl.BlockSpec((rb, vw), lambda b, h, c: (row(b, h, c), rblk + h)),
            pl.BlockSpec((rb, LANES), lambda b, h, c: (row(b, h, c), 0)),
            pl.BlockSpec((n_meta, kw), lambda b, h, c: (0, kb + h)),
            pl.BlockSpec((n_meta, vw), lambda b, h, c: (0, vb + h)),
            pl.BlockSpec((n_meta, LANES), lambda b, h, c: (0, 0)),
            pl.BlockSpec((LANES, kw), lambda b, h, c: (0, h)),
            pl.BlockSpec((1, kw), lambda b, h, c: (0, h)),
            pl.BlockSpec((1, vw), lambda b, h, c: (0, h)),
            pl.BlockSpec((rb, rb), lambda b, h, c: (0, 0)),
        ],
        out_specs=pl.BlockSpec((rb, vw), lambda b, h, c: (row(b, h, c), h)),
        out_shape=jax.ShapeDtypeStruct((rows, GLA_WIDTH), BF16),
        scratch_shapes=[pltpu.VMEM((hps, GLA_DV, GLA_DK), F32)],
        compiler_params=_params(("parallel", "parallel", "arbitrary")),
        name="gla",
    )(p, p, p, p, gd, pm, pm, gdm, wgu, bg, gn, tri)


def _conv_kernel(ca_ref, cb_ref, cam_ref, cbm_ref, w_ref, b_ref, lg_ref, lb_ref, o_ref, u_ref, y_ref, *, tr):
    t = pl.program_id(1)
    n_meta = cam_ref.shape[0]

    @pl.when(t == 0)
    def _():
        um = cam_ref[...].astype(F32) * _sigmoid(cbm_ref[...].astype(F32))
        u_ref[0:CONV_HALO - n_meta, :] = jnp.zeros((CONV_HALO - n_meta, CONV_WIDTH), F32)
        u_ref[CONV_HALO - n_meta:CONV_HALO, :] = um

    @pl.when(t > 0)
    def _():
        u_ref[0:CONV_HALO, :] = u_ref[tr:tr + CONV_HALO, :]

    u_ref[CONV_HALO:CONV_HALO + tr, :] = ca_ref[...].astype(F32) * _sigmoid(cb_ref[...].astype(F32))

    def strip(s, carry):
        cs = pl.ds(pl.multiple_of(s * LANES, LANES), LANES)
        x = u_ref[:, cs]
        n = CONV_HALO + tr
        acc = jnp.zeros((tr, LANES), F32)
        for sh in range(SUBLANES):
            xs = x if sh == 0 else pltpu.roll(x, n - sh, axis=0)
            for j in range(CONV_KERNEL):
                off = CONV_HALO - (CONV_KERNEL - 1) + j
                if off % SUBLANES == sh:
                    acc = acc + w_ref[j:j + 1, cs] * xs[off - sh:off - sh + tr]
        y_ref[:, cs] = acc + b_ref[:, cs]
        return carry

    lax.fori_loop(0, CONV_WIDTH // LANES, strip, 0)

    y = y_ref[...]
    mu = jnp.mean(y, axis=-1, keepdims=True)
    yc = y - mu
    var = jnp.mean(yc * yc, axis=-1, keepdims=True)
    yn = yc * lax.rsqrt(var + EPS) * lg_ref[...] + lb_ref[...]
    o_ref[...] = (yn * _sigmoid(yn)).astype(BF16)


def _conv(p, pm, w_dw, b_dw, ln_g, ln_b, batch, tr):
    rows = p.shape[0]
    nt = rows // batch // tr
    cab = (2 * GLA_KEY_WIDTH + 2 * GLA_WIDTH) // CONV_WIDTH
    return pl.pallas_call(
        functools.partial(_conv_kernel, tr=tr),
        grid=(batch, nt),
        in_specs=[
            pl.BlockSpec((tr, CONV_WIDTH), lambda b, t: (b * nt + t, cab)),
            pl.BlockSpec((tr, CONV_WIDTH), lambda b, t: (b * nt + t, cab + 1)),
            pl.BlockSpec((pm.shape[0], CONV_WIDTH), lambda b, t: (0, cab)),
            pl.BlockSpec((pm.shape[0], CONV_WIDTH), lambda b, t: (0, cab + 1)),
            pl.BlockSpec((CONV_HALO, CONV_WIDTH), lambda b, t: (0, 0)),
            pl.BlockSpec((1, CONV_WIDTH), lambda b, t: (0, 0)),
            pl.BlockSpec((1, CONV_WIDTH), lambda b, t: (0, 0)),
            pl.BlockSpec((1, CONV_WIDTH), lambda b, t: (0, 0)),
        ],
        out_specs=pl.BlockSpec((tr, CONV_WIDTH), lambda b, t: (b * nt + t, 0)),
        out_shape=jax.ShapeDtypeStruct((rows, CONV_WIDTH), BF16),
        scratch_shapes=[pltpu.VMEM((CONV_HALO + tr, CONV_WIDTH), F32), pltpu.VMEM((tr, CONV_WIDTH), F32)],
        compiler_params=_params(("parallel", "arbitrary")),
        name="conv",
    )(p, p, pm, pm, w_dw, b_dw, ln_g, ln_b)


def _out_proj_kernel(g_ref, c_ref, wt_ref, wb_ref, x_ref, o_ref):
    acc = jnp.dot(g_ref[...], wt_ref[...], preferred_element_type=F32)
    acc = acc + jnp.dot(c_ref[...], wb_ref[...], preferred_element_type=F32)
    o_ref[...] = x_ref[...] + acc


def _out_proj(gla_out, conv_out, w_out, x2, tm, tn):
    rows, d = x2.shape
    return pl.pallas_call(
        _out_proj_kernel,
        grid=(rows // tm, d // tn),
        in_specs=[
            pl.BlockSpec((tm, GLA_WIDTH), lambda i, j: (i, 0)),
            pl.BlockSpec((tm, CONV_WIDTH), lambda i, j: (i, 0)),
            pl.BlockSpec((GLA_WIDTH, tn), lambda i, j: (0, j)),
            pl.BlockSpec((CONV_WIDTH, tn), lambda i, j: (GLA_WIDTH // CONV_WIDTH, j)),
            pl.BlockSpec((tm, tn), lambda i, j: (i, j)),
        ],
        out_specs=pl.BlockSpec((tm, tn), lambda i, j: (i, j)),
        out_shape=jax.ShapeDtypeStruct((rows, d), F32),
        compiler_params=_params(("parallel", "arbitrary")),
        name="out_proj",
    )(gla_out, conv_out, w_out, w_out, x2)


def _sort_network(n):
    def merge(lo, hi, r):
        step = r * 2
        if step < hi - lo:
            yield from merge(lo, hi, step)
            yield from merge(lo + r, hi, step)
            yield from ((i, i + r) for i in range(lo + r, hi - r, step))
        else:
            yield (lo, lo + r)

    def sort(lo, hi):
        if hi - lo >= 1:
            mid = lo + (hi - lo) // 2
            yield from sort(lo, mid)
            yield from sort(mid + 1, hi)
            yield from merge(lo, hi, 1)

    return tuple(sort(0, n - 1))


_KEY_SLABS = PEER_NKEYS // SUBLANES
_SLAB_SORT = _sort_network(_KEY_SLABS)


def _top_desc(s_ref, ts, n):
    slabs = [s_ref[k * SUBLANES:(k + 1) * SUBLANES, ts] for k in range(_KEY_SLABS)]
    for i, j in _SLAB_SORT:
        slabs[i], slabs[j] = jnp.maximum(slabs[i], slabs[j]), jnp.minimum(slabs[i], slabs[j])
    vals = []
    for t in range(n):
        m = jnp.max(slabs[0], axis=0, keepdims=True)
        vals.append(m)
        hit = slabs[0] == m
        for lvl in range(min(_KEY_SLABS, n - 1 - t)):
            below = slabs[lvl + 1] if lvl + 1 < _KEY_SLABS else jnp.full_like(slabs[lvl], -jnp.inf)
            slabs[lvl] = jnp.where(hit, below, slabs[lvl])
    return vals


def _top_products(p1, p2_rows, n):
    rows = p1.shape[0]
    rid = lax.broadcasted_iota(jnp.int32, (rows, LANES), 0)
    levels = []
    for b in range(n):
        nrow = rows if b == 0 else SUBLANES
        levels.append(jnp.where(rid[0:nrow] < n // (b + 1), p1[0:nrow] * p2_rows[b], -1.0))
    out = []
    for t in range(n):
        head = levels[0]
        m = jnp.max(head, axis=0, keepdims=True)
        out.append(m)
        left = n - 1 - t
        if left > 0:
            hit = head == m
            hit0 = hit[0:SUBLANES]
            levels[0] = jnp.concatenate([jnp.where(hit0, levels[1], head[0:SUBLANES]),
                                         jnp.where(hit[SUBLANES:], -1.0, head[SUBLANES:])], axis=0)
            for lvl in range(1, left):
                below = levels[lvl + 1] if lvl + 1 < n else jnp.full_like(levels[lvl], -1.0)
                levels[lvl] = jnp.where(hit0, below, levels[lvl])
    return out


def _peer_score_kernel(h_ref, g_ref, wq_ref, k1_ref, k2_ref, xn_ref, e1_ref, e2_ref, thr_ref, s1_ref, s2_ref):
    @pl.when(pl.program_id(1) == 0)
    def _():
        xn_ref[...] = _rms_norm(h_ref[...], g_ref[...]).astype(BF16)

    q = jnp.dot(xn_ref[...], wq_ref[...], preferred_element_type=F32).astype(BF16)
    s1_ref[...] = lax.dot_general(k1_ref[0], q[:, :PEER_HALF], NT_DIMS, preferred_element_type=F32)
    s2_ref[...] = lax.dot_general(k2_ref[0], q[:, PEER_HALF:], NT_DIMS, preferred_element_type=F32)
    n = PEER_TOPK + 1
    pad = 3 * SUBLANES - n
    for tb in range(s1_ref.shape[1] // LANES):
        ts = slice(tb * LANES, (tb + 1) * LANES)
        v1 = _top_desc(s1_ref, ts, n)
        v2 = _top_desc(s2_ref, ts, n)
        p1 = jnp.exp(jnp.concatenate(v1 + [v1[-1]] * pad, axis=0) - v1[0])
        p2 = jnp.exp(jnp.concatenate(v2 + [v2[-1]] * pad, axis=0) - v2[0])
        w = _top_products(p1, [p2[b:b + 1] for b in range(n)], n)
        z = w[0]
        for t in range(1, PEER_TOPK):
            z = z + w[t]
        rz = 1.0 / z
        e1n = jnp.exp(s1_ref[:, ts] - v1[0]) * rz
        grp = e1_ref.shape[3]
        for kb in range(PEER_NKEYS // grp):
            e1_ref[kb, 0, tb] = e1n[kb * grp:(kb + 1) * grp]
        e2_ref[0, :, ts] = jnp.exp(s2_ref[:, ts] - v2[0])
        thr_ref[0, :, ts] = 0.5 * (w[PEER_TOPK - 1] + w[PEER_TOPK]) * rz


def _peer_score(h1, g2, wq, k1, k2, tm, grp):
    rows, d = h1.shape
    qw = 2 * PEER_HALF
    return pl.pallas_call(
        _peer_score_kernel,
        grid=(rows // tm, PEER_HEADS),
        in_specs=[
            pl.BlockSpec((tm, d), lambda i, h: (i, 0)),
            pl.BlockSpec((1, d), lambda i, h: (0, 0)),
            pl.BlockSpec((d, qw), lambda i, h: (0, h)),
            pl.BlockSpec((1, PEER_NKEYS, PEER_HALF), lambda i, h: (h, 0, 0)),
            pl.BlockSpec((1, PEER_NKEYS, PEER_HALF), lambda i, h: (h, 0, 0)),
        ],
        out_specs=[
            pl.BlockSpec((tm, d), lambda i, h: (i, 0)),
            pl.BlockSpec((PEER_NKEYS // grp, 1, tm // LANES, grp, LANES), lambda i, h: (0, h, i, 0, 0)),
            pl.BlockSpec((1, PEER_NKEYS, tm), lambda i, h: (h, 0, i)),
            pl.BlockSpec((1, 1, tm), lambda i, h: (h, 0, i)),
        ],
        out_shape=[
            jax.ShapeDtypeStruct((rows, d), BF16),
            jax.ShapeDtypeStruct((PEER_NKEYS // grp, PEER_HEADS, rows // LANES, grp, LANES), F32),
            jax.ShapeDtypeStruct((PEER_HEADS, PEER_NKEYS, rows), F32),
            jax.ShapeDtypeStruct((PEER_HEADS, 1, rows), F32),
        ],
        scratch_shapes=[pltpu.VMEM((PEER_NKEYS, tm), F32), pltpu.VMEM((PEER_NKEYS, tm), F32)],
        compiler_params=_params(("parallel", "arbitrary")),
        name="peer_score",
    )(h1, g2, wq, k1, k2)


def _peer_expert_up(xn_ref, u_ref, at_ref, piece):
    ts = slice(piece * PEER_UPIECE, (piece + 1) * PEER_UPIECE)
    r = lax.dot_general(u_ref[...], xn_ref[ts, :], NT_DIMS, preferred_element_type=F32)
    at_ref[:, ts] = r
    return r[r.shape[0] - SUBLANES:, PEER_UPIECE - LANES:]


def _after(row, anchor, zero_ref):
    z = pltpu.bitcast(anchor, jnp.int32) & zero_ref[...]
    return row + pltpu.bitcast(z, F32)[0:1, :]


def _peer_expert_gate(c, tb, hg_cur, e1_ref, e2_ref, thr_ref, at_ref, anchor, zero_ref):
    rs = slice(c * PEER_NKEYS, (c + 1) * PEER_NKEYS)
    ts = slice(tb * LANES, (tb + 1) * LANES)
    g = jnp.zeros((PEER_NKEYS, LANES), F32)
    for h in range(PEER_HEADS):
        e1_row = e1_ref[0, h, tb, c:c + 1, :]
        if anchor is not None:
            e1_row = _after(e1_row, anchor, zero_ref)
        w = e1_row * e2_ref[h, :, ts]
        g = g + jnp.where(w >= thr_ref[h, :, ts], w, 0.0)
    a = at_ref[rs, ts]
    gelu = 0.5 * a * (1.0 + lax.erf(a * (2.0 ** -0.5)))
    hg_cur[ts, rs] = (gelu * g).T.astype(BF16)


def _peer_expert_down(hg_prev, v_ref, o_ref, piece):
    cs = slice(piece * PEER_DPIECE, (piece + 1) * PEER_DPIECE)
    r = jnp.dot(hg_prev[...], v_ref[:, cs], preferred_element_type=F32)
    o_ref[:, cs] += r
    return r[r.shape[0] - SUBLANES:, PEER_DPIECE - LANES:]


def _peer_dense_kernel(xn_ref, u_ref, v_ref, e1_ref, e2_ref, thr_ref, h_ref, gf_ref, zero_ref, o_ref,
                       at_ref, hg0_ref, hg1_ref, *, te, n_tiles):
    e = pl.program_id(1)
    n_sub = te // PEER_NKEYS
    tm = xn_ref.shape[0]
    n_up = tm // PEER_UPIECE
    n_down = o_ref.shape[1] // PEER_DPIECE
    hg = (hg0_ref, hg1_ref)
    tiles = [(c, tb) for c in range(n_sub) for tb in range(tm // LANES)]

    def body(par, up, down, first=False, last=False):
        if first:
            o_ref[...] = jnp.zeros_like(o_ref)
        todo = list(tiles) if up else []
        anchor = None
        if up:
            for k in range(n_up):
                _peer_expert_up(xn_ref, u_ref, at_ref, k)
        for k in range(n_down if down else 0):
            done = _peer_expert_down(hg[1 - par], v_ref, o_ref, k)
            n_now = len(todo) - round(len(tiles) * (1.0 - (k + 1) / n_down)) if todo else 0
            for c, tb in todo[:n_now]:
                _peer_expert_gate(c, tb, hg[par], e1_ref, e2_ref, thr_ref, at_ref, anchor, zero_ref)
            todo = todo[n_now:]
            anchor = done
        for c, tb in todo:
            _peer_expert_gate(c, tb, hg[par], e1_ref, e2_ref, thr_ref, at_ref, None, zero_ref)
        if last:
            o_ref[...] = _rms_norm(h_ref[...] + o_ref[...], gf_ref[...])

    pl.when(e == 0)(functools.partial(body, 0, True, False, first=True))
    for par in (0, 1):
        pl.when((e >= 1) & (e < n_tiles) & (e % 2 == par))(functools.partial(body, par, True, True))
    pl.when(e == n_tiles)(functools.partial(body, n_tiles % 2, False, True, last=True))


def _peer_dense(xn2, u, v, e1, e2, thr, h1, gf, tm, te):
    rows, d = h1.shape
    n_tiles = u.shape[0] // te
    zero = jnp.zeros((SUBLANES, LANES), jnp.int32)
    return pl.pallas_call(
        functools.partial(_peer_dense_kernel, te=te, n_tiles=n_tiles),
        grid=(rows // tm, n_tiles + 1),
        in_specs=[
            pl.BlockSpec((tm, d), lambda i, e: (i, 0), pipeline_mode=pl.Buffered(1)),
            pl.BlockSpec((te, d), lambda i, e: (jnp.minimum(e, n_tiles - 1), 0)),
            pl.BlockSpec((te, d), lambda i, e: (jnp.maximum(e - 1, 0), 0)),
            pl.BlockSpec((1, PEER_HEADS, tm // LANES, te // PEER_NKEYS, LANES),
                         lambda i, e: (jnp.minimum(e, n_tiles - 1), 0, i, 0, 0)),
            pl.BlockSpec((PEER_HEADS, PEER_NKEYS, tm), lambda i, e: (0, 0, i), pipeline_mode=pl.Buffered(1)),
            pl.BlockSpec((PEER_HEADS, 1, tm), lambda i, e: (0, 0, i)),
            pl.BlockSpec((tm, d), lambda i, e: (i, 0), pipeline_mode=pl.Buffered(1)),
            pl.BlockSpec((1, d), lambda i, e: (0, 0)),
            pl.BlockSpec((SUBLANES, LANES), lambda i, e: (0, 0)),
        ],
        out_specs=pl.BlockSpec((tm, d), lambda i, e: (i, 0), pipeline_mode=pl.Buffered(1)),
        out_shape=jax.ShapeDtypeStruct((rows, d), F32),
        scratch_shapes=[pltpu.VMEM((te, tm), F32), pltpu.VMEM((tm, te), BF16), pltpu.VMEM((tm, te), BF16)],
        compiler_params=_params(("parallel", "arbitrary")),
        name="peer_dense",
    )(xn2, u, v, e1, e2, thr, h1, gf, zero)


def _tile(n, pref):
    t = min(n, pref)
    assert n % t == 0, (n, t)
    return t


def kernel(x, meta_tokens, norm1_g, w_in, w_gate_up, b_gate, gla_norm_g, w_dw, b_dw, conv_ln_g, conv_ln_b,
           w_out, norm2_g, peer_wq, peer_keys1, peer_keys2, peer_u, peer_v, final_norm_g):
    batch, seq, d = x.shape
    rows = batch * seq
    assert norm1_g.shape[0] == 1, "single-layer block"
    assert seq % CHUNK == 0 and meta_tokens.shape[0] <= CONV_HALO and (CHUNK - meta_tokens.shape[0]) >= 0
    x2 = x.reshape(rows, d)

    r_end = 2 * GLA_KEY_WIDTH + 2 * GLA_WIDTH
    gd_end = r_end + GLA_GATE_RANK
    w_in_t = w_in[0].T.astype(BF16)
    w_gd_t = jnp.pad(w_in_t[r_end:gd_end], ((0, LANES - GLA_GATE_RANK), (0, 0)))
    wgu = jnp.pad(w_gate_up[0], ((0, LANES - GLA_GATE_RANK), (0, 0))).astype(BF16)
    g1 = norm1_g[0][None]

    tm = _tile(rows, 512)
    te = 512
    p, gd = _in_proj(x2, g1, w_in_t, w_gd_t, tm, 1024, r_end, GLA_GATE_RANK)
    pm, gdm = _in_proj(meta_tokens.astype(F32), g1, w_in_t, w_gd_t, meta_tokens.shape[0], 1024,
                       r_end, GLA_GATE_RANK)

    gla_out = _gla(p, gd, pm, gdm, wgu, b_gate[0][None], gla_norm_g[0][None], batch, _tile(seq, 512), 2)
    w_dw_p = jnp.pad(w_dw[0], ((0, CONV_HALO - CONV_KERNEL), (0, 0)))
    conv_out = _conv(p, pm, w_dw_p, b_dw[0][None], conv_ln_g[0][None], conv_ln_b[0][None], batch,
                     _tile(seq, 128))
    h1 = _out_proj(gla_out, conv_out, w_out[0].astype(BF16), x2, tm, _tile(d, 1024))

    xn2, e1, e2, thr = _peer_score(h1, norm2_g[0][None], peer_wq[0].astype(BF16),
                                   peer_keys1[0].astype(BF16), peer_keys2[0].astype(BF16), tm, te // PEER_NKEYS)
    out = _peer_dense(xn2, peer_u[0].astype(BF16), peer_v[0].astype(BF16), e1, e2, thr, h1,
                      final_norm_g[None], tm, te)
    return out.reshape(batch, seq, d)
```

```python
import functools

import jax
import jax.numpy as jnp
from jax import lax
from jax.experimental import pallas as pl
from jax.experimental.pallas import tpu as pltpu

F32 = jnp.float32
BF16 = jnp.bfloat16

EPS = 1e-6
CHUNK = 64
GLA_HEADS = 8
GLA_DK = 128
GLA_DV = 256
GLA_KEY_WIDTH = GLA_HEADS * GLA_DK
GLA_WIDTH = GLA_HEADS * GLA_DV
GLA_GATE_RANK = 16
GLA_TAU = 16.0
CONV_WIDTH = 2048
CONV_KERNEL = 31
CONV_HALO = 32
PEER_HEADS = 8
PEER_NKEYS = 128
PEER_HALF = 128
PEER_TOPK = 16
PEER_UPIECE = 256
PEER_DPIECE = 256
PEER_GATE_SPAN = 1.0

LANES = 128
SUBLANES = 8
CAST_ROWS = 128
VMEM_LIMIT = 56 * 1024 * 1024

NT_DIMS = (((1,), (1,)), ((), ()))
TN_DIMS = (((0,), (0,)), ((), ()))


def _sigmoid(x):
    return 1.0 / (1.0 + jnp.exp(-x))


def _rms_norm(xf, g):
    ms = jnp.mean(xf * xf, axis=-1, keepdims=True)
    return xf * lax.rsqrt(ms + EPS) * g


def _params(sem, flags=None):
    return pltpu.CompilerParams(dimension_semantics=sem, vmem_limit_bytes=VMEM_LIMIT, flags=flags)


def _cast_specs(casts, n_i, n_j):
    specs = []
    for k, c in enumerate(casts):
        n_blk = c.shape[0] // CAST_ROWS
        assert c.shape[0] % CAST_ROWS == 0 and (k + 1) * n_blk <= n_i * n_j
        specs.append(pl.BlockSpec(
            (CAST_ROWS, c.shape[1]),
            lambda i, j, k=k, n_blk=n_blk: (jnp.clip(i * n_j + j - k * n_blk, 0, n_blk - 1), 0)))
    return specs


def _in_proj_kernel(x_ref, g_ref, w_ref, wgd_ref, *rest, n_cast):
    cast_in = rest[:n_cast]
    p_ref, gd_ref = rest[n_cast:n_cast + 2]
    cast_out = rest[n_cast + 2:2 * n_cast + 2]
    xn_ref = rest[2 * n_cast + 2]

    @pl.when(pl.program_id(1) == 0)
    def _():
        xn = _rms_norm(x_ref[...], g_ref[...]).astype(BF16)
        xn_ref[...] = xn
        gd_ref[...] = lax.dot_general(xn, wgd_ref[...], NT_DIMS, preferred_element_type=F32).astype(BF16)

    p_ref[...] = lax.dot_general(xn_ref[...], w_ref[...], NT_DIMS, preferred_element_type=F32).astype(BF16)
    for ci, co in zip(cast_in, cast_out):
        co[...] = ci[...].astype(BF16)


def _in_proj(x2, g, w_t, w_gd_t, tm, tn, skip_at, skip_n, casts=()):
    rows, d = x2.shape
    n = w_t.shape[0] - skip_n
    assert skip_at % tn == 0 and n % tn == 0
    n_i, n_j = rows // tm, n // tn
    cast_specs = _cast_specs(casts, n_i, n_j)
    return pl.pallas_call(
        functools.partial(_in_proj_kernel, n_cast=len(casts)),
        grid=(n_i, n_j),
        in_specs=[
            pl.BlockSpec((tm, d), lambda i, j: (i, 0)),
            pl.BlockSpec((1, d), lambda i, j: (0, 0)),
            pl.BlockSpec((pl.Element(tn), pl.Element(d)),
                         lambda i, j: (pl.multiple_of(j * tn + jnp.where(j * tn >= skip_at, skip_n, 0), skip_n), 0)),
            pl.BlockSpec((LANES, d), lambda i, j: (0, 0)),
        ] + cast_specs,
        out_specs=[
            pl.BlockSpec((tm, tn), lambda i, j: (i, j)),
            pl.BlockSpec((tm, LANES), lambda i, j: (i, 0)),
        ] + cast_specs,
        out_shape=[
            jax.ShapeDtypeStruct((rows, n), BF16),
            jax.ShapeDtypeStruct((rows, LANES), BF16),
        ] + [jax.ShapeDtypeStruct(c.shape, BF16) for c in casts],
        scratch_shapes=[pltpu.VMEM((tm, d), BF16)],
        compiler_params=_params(("arbitrary", "arbitrary")),
        name="in_proj",
    )(x2, g, w_t, w_gd_t, *casts)


def _gla_decayed_keys(k, gd, tri, wgu, bg):
    z = jnp.dot(gd, wgu, preferred_element_type=F32) + bg
    la = (jnp.minimum(z, 0.0) - jnp.log1p(jnp.exp(-jnp.abs(z)))) * (1.0 / GLA_TAU)
    la_hi = la.astype(BF16)
    la_lo = (la - la_hi.astype(F32)).astype(BF16)
    rev = (jnp.dot(tri, la_hi, preferred_element_type=F32)
           + jnp.dot(tri, la_lo, preferred_element_type=F32))
    return (k.astype(F32) * jnp.exp(rev)).astype(BF16), la


def _gla_head_blocks(v, kdec, n_heads):
    full = lax.dot_general(v, kdec, TN_DIMS, preferred_element_type=F32)
    return [full[h * GLA_DV:(h + 1) * GLA_DV, h * GLA_DK:(h + 1) * GLA_DK] for h in range(n_heads)]


def _gla_kernel(q_ref, k_ref, v_ref, r_ref, gd_ref, km_ref, vm_ref, gdm_ref, wgu_ref, bg_ref, gn_ref, tri_ref,
                o_ref, st_ref, *, n_chunks, n_heads):
    wgu = wgu_ref[...]
    bg = bg_ref[...]
    n_meta = km_ref.shape[0]

    @pl.when(pl.program_id(2) == 0)
    def _():
        kdec_m, _ = _gla_decayed_keys(km_ref[...], gdm_ref[...], tri_ref[0:n_meta, 0:n_meta], wgu, bg)
        for h, blk in enumerate(_gla_head_blocks(vm_ref[...], kdec_m, n_heads)):
            st_ref[h] = blk

    kdec, la = _gla_decayed_keys(k_ref[...], gd_ref[...], tri_ref[...], wgu, bg)
    sts = [st_ref[h] for h in range(n_heads)]
    zero = jnp.zeros((GLA_DV, GLA_DK), BF16)
    for c in range(n_chunks):
        sl = slice(c * CHUNK, (c + 1) * CHUNK)
        decay = jnp.exp(jnp.sum(la[sl], axis=0, keepdims=True))
        kvs = _gla_head_blocks(v_ref[sl, :], kdec[sl], n_heads)
        rows = []
        for h in range(n_heads):
            sts[h] = sts[h] * decay[:, h * GLA_DK:(h + 1) * GLA_DK] + kvs[h]
            rows.append(jnp.concatenate([zero] * h + [sts[h].astype(BF16)] + [zero] * (n_heads - 1 - h), axis=1))
        s_blockdiag = jnp.concatenate(rows, axis=0)
        o = lax.dot_general(q_ref[sl, :], s_blockdiag, NT_DIMS,
                            preferred_element_type=F32) * (GLA_DK ** -0.5)
        on = jnp.concatenate([_rms_norm(o[:, h * GLA_DV:(h + 1) * GLA_DV], gn_ref[:, h * GLA_DV:(h + 1) * GLA_DV])
                              for h in range(n_heads)], axis=1)
        r = r_ref[sl, :].astype(F32)
        o_ref[sl, :] = (on * (r * _sigmoid(r))).astype(BF16)
    for h in range(n_heads):
        st_ref[h] = sts[h]


def _gla(p, gd, pm, gdm, wgu, bg, gn, batch, rb, hps):
    rows = p.shape[0]
    nrb = rows // batch // rb
    kw, vw = hps * GLA_DK, hps * GLA_DV
    kb = GLA_KEY_WIDTH // kw
    vb = 2 * GLA_KEY_WIDTH // vw
    rblk = (2 * GLA_KEY_WIDTH + GLA_WIDTH) // vw
    n_meta = pm.shape[0]
    frame = jnp.arange(rb)
    tri = ((frame[:, None] // CHUNK == frame[None, :] // CHUNK) & (frame[None, :] > frame[:, None])).astype(BF16)
    row = lambda b, h, c: b * nrb + c
    return pl.pallas_call(
        functools.partial(_gla_kernel, n_chunks=rb // CHUNK, n_heads=hps),
        grid=(batch, GLA_HEADS // hps, nrb),
        in_specs=[
            pl.BlockSpec((rb, kw), lambda b, h, c: (row(b, h, c), h)),
            pl.BlockSpec((rb, kw), lambda b, h, c: (row(b, h, c), kb + h)),
            pl.BlockSpec((rb, vw), lambda b, h, c: (row(b, h, c), vb + h)),
            pl.BlockSpec((rb, vw), lambda b, h, c: (row(b, h, c), rblk + h)),
            pl.BlockSpec((rb, LANES), lambda b, h, c: (row(b, h, c), 0)),
            pl.BlockSpec((n_meta, kw), lambda b, h, c: (0, kb + h)),
            pl.BlockSpec((n_meta, vw), lambda b, h, c: (0, vb + h)),
            pl.BlockSpec((n_meta, LANES), lambda b, h, c: (0, 0)),
            pl.BlockSpec((LANES, kw), lambda b, h, c: (0, h)),
            pl.BlockSpec((1, kw), lambda b, h, c: (0, h)),
            pl.BlockSpec((1, vw), lambda b, h, c: (0, h)),
            pl.BlockSpec((rb, rb), lambda b, h, c: (0, 0)),
        ],
        out_specs=pl.BlockSpec((rb, vw), lambda b, h, c: (row(b, h, c), h)),
        out_shape=jax.ShapeDtypeStruct((rows, GLA_WIDTH), BF16),
        scratch_shapes=[pltpu.VMEM((hps, GLA_DV, GLA_DK), F32)],
        compiler_params=_params(("parallel", "parallel", "arbitrary")),
        name="gla",
    )(p, p, p, p, gd, pm, pm, gdm, wgu, bg, gn, tri)


def _conv_kernel(ca_ref, cb_ref, cam_ref, cbm_ref, w_ref, b_ref, lg_ref, lb_ref, o_ref, u_ref, y_ref, *, tr):
    t = pl.program_id(1)
    n_meta = cam_ref.shape[0]

    @pl.when(t == 0)
    def _():
        um = cam_ref[...].astype(F32) * _sigmoid(cbm_ref[...].astype(F32))
        u_ref[0:CONV_HALO - n_meta, :] = jnp.zeros((CONV_HALO - n_meta, CONV_WIDTH), F32)
        u_ref[CONV_HALO - n_meta:CONV_HALO, :] = um

    @pl.when(t > 0)
    def _():
        u_ref[0:CONV_HALO, :] = u_ref[tr:tr + CONV_HALO, :]

    u_ref[CONV_HALO:CONV_HALO + tr, :] = ca_ref[...].astype(F32) * _sigmoid(cb_ref[...].astype(F32))

    def strip(s, carry):
        cs = pl.ds(pl.multiple_of(s * LANES, LANES), LANES)
        x = u_ref[:, cs]
        n = CONV_HALO + tr
        acc = jnp.zeros((tr, LANES), F32)
        for sh in range(SUBLANES):
            xs = x if sh == 0 else pltpu.roll(x, n - sh, axis=0)
            for j in range(CONV_KERNEL):
                off = CONV_HALO - (CONV_KERNEL - 1) + j
                if off % SUBLANES == sh:
                    acc = acc + w_ref[j:j + 1, cs] * xs[off - sh:off - sh + tr]
        y_ref[:, cs] = acc + b_ref[:, cs]
        return carry

    lax.fori_loop(0, CONV_WIDTH // LANES, strip, 0)

    y = y_ref[...]
    mu = jnp.mean(y, axis=-1, keepdims=True)
    yc = y - mu
    var = jnp.mean(yc * yc, axis=-1, keepdims=True)
    yn = yc * lax.rsqrt(var + EPS) * lg_ref[...] + lb_ref[...]
    o_ref[...] = (yn * _sigmoid(yn)).astype(BF16)


def _conv(p, pm, w_dw, b_dw, ln_g, ln_b, batch, tr):
    rows = p.shape[0]
    nt = rows // batch // tr
    cab = (2 * GLA_KEY_WIDTH + 2 * GLA_WIDTH) // CONV_WIDTH
    return pl.pallas_call(
        functools.partial(_conv_kernel, tr=tr),
        grid=(batch, nt),
        in_specs=[
            pl.BlockSpec((tr, CONV_WIDTH), lambda b, t: (b * nt + t, cab)),
            pl.BlockSpec((tr, CONV_WIDTH), lambda b, t: (b * nt + t, cab + 1)),
            pl.BlockSpec((pm.shape[0], CONV_WIDTH), lambda b, t: (0, cab)),
            pl.BlockSpec((pm.shape[0], CONV_WIDTH), lambda b, t: (0, cab + 1)),
            pl.BlockSpec((CONV_HALO, CONV_WIDTH), lambda b, t: (0, 0)),
            pl.BlockSpec((1, CONV_WIDTH), lambda b, t: (0, 0)),
            pl.BlockSpec((1, CONV_WIDTH), lambda b, t: (0, 0)),
            pl.BlockSpec((1, CONV_WIDTH), lambda b, t: (0, 0)),
        ],
        out_specs=pl.BlockSpec((tr, CONV_WIDTH), lambda b, t: (b * nt + t, 0)),
        out_shape=jax.ShapeDtypeStruct((rows, CONV_WIDTH), BF16),
        scratch_shapes=[pltpu.VMEM((CONV_HALO + tr, CONV_WIDTH), F32), pltpu.VMEM((tr, CONV_WIDTH), F32)],
        compiler_params=_params(("parallel", "arbitrary")),
        name="conv",
    )(p, p, pm, pm, w_dw, b_dw, ln_g, ln_b)


def _out_proj_kernel(g_ref, c_ref, wt_ref, wb_ref, x_ref, *rest, n_cast):
    cast_in, o_ref, cast_out = rest[:n_cast], rest[n_cast], rest[n_cast + 1:]
    acc = jnp.dot(g_ref[...], wt_ref[...], preferred_element_type=F32)
    acc = acc + jnp.dot(c_ref[...], wb_ref[...], preferred_element_type=F32)
    o_ref[...] = x_ref[...] + acc
    for ci, co in zip(cast_in, cast_out):
        co[...] = ci[...].astype(BF16)


def _out_proj(gla_out, conv_out, w_out, x2, tm, tn, casts=()):
    rows, d = x2.shape
    cast_specs = _cast_specs(casts, rows // tm, d // tn)
    return pl.pallas_call(
        functools.partial(_out_proj_kernel, n_cast=len(casts)),
        grid=(rows // tm, d // tn),
        in_specs=[
            pl.BlockSpec((tm, GLA_WIDTH), lambda i, j: (i, 0)),
            pl.BlockSpec((tm, CONV_WIDTH), lambda i, j: (i, 0)),
            pl.BlockSpec((GLA_WIDTH, tn), lambda i, j: (0, j)),
            pl.BlockSpec((CONV_WIDTH, tn), lambda i, j: (GLA_WIDTH // CONV_WIDTH, j)),
            pl.BlockSpec((tm, tn), lambda i, j: (i, j)),
        ] + cast_specs,
        out_specs=[pl.BlockSpec((tm, tn), lambda i, j: (i, j))] + cast_specs,
        out_shape=[jax.ShapeDtypeStruct((rows, d), F32)] + [jax.ShapeDtypeStruct(c.shape, BF16) for c in casts],
        compiler_params=_params(("arbitrary", "arbitrary")),
        name="out_proj",
    )(gla_out, conv_out, w_out, w_out, x2, *casts)


def _sort_network(n):
    def merge(lo, hi, r):
        step = r * 2
        if step < hi - lo:
            yield from merge(lo, hi, step)
            yield from merge(lo + r, hi, step)
            yield from ((i, i + r) for i in range(lo + r, hi - r, step))
        else:
            yield (lo, lo + r)

    def sort(lo, hi):
        if hi - lo >= 1:
            mid = lo + (hi - lo) // 2
            yield from sort(lo, mid)
            yield from sort(mid + 1, hi)
            yield from merge(lo, hi, 1)

    return tuple(sort(0, n - 1))


_KEY_SLABS = PEER_NKEYS // SUBLANES
_SLAB_SORT = _sort_network(_KEY_SLABS)


def _top_desc(s_ref, ts, n):
    slabs = [s_ref[k * SUBLANES:(k + 1) * SUBLANES, ts] for k in range(_KEY_SLABS)]
    for i, j in _SLAB_SORT:
        slabs[i], slabs[j] = jnp.maximum(slabs[i], slabs[j]), jnp.minimum(slabs[i], slabs[j])
    vals = []
    for t in range(n):
        m = jnp.max(slabs[0], axis=0, keepdims=True)
        vals.append(m)
        hit = slabs[0] == m
        for lvl in range(min(_KEY_SLABS, n - 1 - t)):
            below = slabs[lvl + 1] if lvl + 1 < _KEY_SLABS else jnp.full_like(slabs[lvl], -jnp.inf)
            slabs[lvl] = jnp.where(hit, below, slabs[lvl])
    return vals


def _top_products(p1, p2_rows, n):
    rows = p1.shape[0]
    rid = lax.broadcasted_iota(jnp.int32, (rows, LANES), 0)
    levels = []
    for b in range(n):
        nrow = rows if b == 0 else SUBLANES
        levels.append(jnp.where(rid[0:nrow] < n // (b + 1), p1[0:nrow] * p2_rows[b], -1.0))
    out = []
    for t in range(n):
        head = levels[0]
        m = jnp.max(head, axis=0, keepdims=True)
        out.append(m)
        left = n - 1 - t
        if left > 0:
            hit = head == m
            hit0 = hit[0:SUBLANES]
            levels[0] = jnp.concatenate([jnp.where(hit0, levels[1], head[0:SUBLANES]),
                                         jnp.where(hit[SUBLANES:], -1.0, head[SUBLANES:])], axis=0)
            for lvl in range(1, left):
                below = levels[lvl + 1] if lvl + 1 < n else jnp.full_like(levels[lvl], -1.0)
                levels[lvl] = jnp.where(hit0, below, levels[lvl])
    return out


def _peer_score_kernel(h_ref, g_ref, wq_ref, k1_ref, k2_ref, xn_ref, e1_ref, e2_ref, thr_ref, s1_ref, s2_ref):
    @pl.when(pl.program_id(1) == 0)
    def _():
        xn_ref[...] = _rms_norm(h_ref[...], g_ref[...]).astype(BF16)

    q = jnp.dot(xn_ref[...], wq_ref[...], preferred_element_type=F32).astype(BF16)
    s1_ref[...] = lax.dot_general(k1_ref[0], q[:, :PEER_HALF], NT_DIMS, preferred_element_type=F32)
    s2_ref[...] = lax.dot_general(k2_ref[0], q[:, PEER_HALF:], NT_DIMS, preferred_element_type=F32)
    n = PEER_TOPK + 1
    pad = 3 * SUBLANES - n
    for tb in range(s1_ref.shape[1] // LANES):
        ts = slice(tb * LANES, (tb + 1) * LANES)
        v1 = _top_desc(s1_ref, ts, n)
        v2 = _top_desc(s2_ref, ts, n)
        p1 = jnp.exp(jnp.concatenate(v1 + [v1[-1]] * pad, axis=0) - v1[0])
        p2 = jnp.exp(jnp.concatenate(v2 + [v2[-1]] * pad, axis=0) - v2[0])
        w = _top_products(p1, [p2[b:b + 1] for b in range(n)], n)
        z = w[0]
        for t in range(1, PEER_TOPK):
            z = z + w[t]
        rz = 1.0 / z
        e1n = jnp.exp(s1_ref[:, ts] - v1[0]) * rz
        grp = e1_ref.shape[3]
        for kb in range(PEER_NKEYS // grp):
            e1_ref[kb, 0, tb] = e1n[kb * grp:(kb + 1) * grp]
        e2_ref[0, :, ts] = jnp.exp(s2_ref[:, ts] - v2[0])
        thr_ref[0, :, ts] = 0.5 * (w[PEER_TOPK - 1] + w[PEER_TOPK]) * rz


def _peer_score(h1, g2, wq, k1, k2, tm, grp):
    rows, d = h1.shape
    qw = 2 * PEER_HALF
    return pl.pallas_call(
        _peer_score_kernel,
        grid=(rows // tm, PEER_HEADS),
        in_specs=[
            pl.BlockSpec((tm, d), lambda i, h: (i, 0)),
            pl.BlockSpec((1, d), lambda i, h: (0, 0)),
            pl.BlockSpec((d, qw), lambda i, h: (0, h)),
            pl.BlockSpec((1, PEER_NKEYS, PEER_HALF), lambda i, h: (h, 0, 0)),
            pl.BlockSpec((1, PEER_NKEYS, PEER_HALF), lambda i, h: (h, 0, 0)),
        ],
        out_specs=[
            pl.BlockSpec((tm, d), lambda i, h: (i, 0)),
            pl.BlockSpec((PEER_NKEYS // grp, 1, tm // LANES, grp, LANES), lambda i, h: (0, h, i, 0, 0)),
            pl.BlockSpec((1, PEER_NKEYS, tm), lambda i, h: (h, 0, i)),
            pl.BlockSpec((1, 1, tm), lambda i, h: (h, 0, i)),
        ],
        out_shape=[
            jax.ShapeDtypeStruct((rows, d), BF16),
            jax.ShapeDtypeStruct((PEER_NKEYS // grp, PEER_HEADS, rows // LANES, grp, LANES), F32),
            jax.ShapeDtypeStruct((PEER_HEADS, PEER_NKEYS, rows), F32),
            jax.ShapeDtypeStruct((PEER_HEADS, 1, rows), F32),
        ],
        scratch_shapes=[pltpu.VMEM((PEER_NKEYS, tm), F32), pltpu.VMEM((PEER_NKEYS, tm), F32)],
        compiler_params=_params(("parallel", "arbitrary")),
        name="peer_score",
    )(h1, g2, wq, k1, k2)


def _peer_expert_up(xn_ref, u_ref, at_ref, piece):
    ts = slice(piece * PEER_UPIECE, (piece + 1) * PEER_UPIECE)
    r = lax.dot_general(u_ref[...], xn_ref[ts, :], NT_DIMS, preferred_element_type=F32)
    at_ref[:, ts] = r
    return r[r.shape[0] - SUBLANES:, PEER_UPIECE - LANES:]


def _after(row, anchor, zero_ref):
    z = pltpu.bitcast(anchor, jnp.int32) & zero_ref[...]
    return row + pltpu.bitcast(z, F32)[0:1, :]


def _peer_expert_gate(c, tb, hg_cur, e1_ref, e2_ref, thr_ref, at_ref, anchor, zero_ref):
    rs = slice(c * PEER_NKEYS, (c + 1) * PEER_NKEYS)
    ts = slice(tb * LANES, (tb + 1) * LANES)
    g = jnp.zeros((PEER_NKEYS, LANES), F32)
    for h in range(PEER_HEADS):
        e1_row = e1_ref[0, h, tb, c:c + 1, :]
        if anchor is not None:
            e1_row = _after(e1_row, anchor, zero_ref)
        w = e1_row * e2_ref[h, :, ts]
        g = g + jnp.where(w >= thr_ref[h, :, ts], w, 0.0)
    a = at_ref[rs, ts]
    gelu = 0.5 * a * (1.0 + lax.erf(a * (2.0 ** -0.5)))
    hg_cur[ts, rs] = (gelu * g).T.astype(BF16)


def _peer_expert_down(hg_prev, v_ref, o_ref, piece):
    cs = slice(piece * PEER_DPIECE, (piece + 1) * PEER_DPIECE)
    r = jnp.dot(hg_prev[...], v_ref[:, cs], preferred_element_type=F32)
    o_ref[:, cs] += r
    return r[r.shape[0] - SUBLANES:, PEER_DPIECE - LANES:]


def _peer_dense_kernel(xn_ref, u_ref, v_ref, e1_ref, e2_ref, thr_ref, h_ref, gf_ref, zero_ref, o_ref,
                       at_ref, hg0_ref, hg1_ref, *, te, n_tiles):
    e = pl.program_id(1)
    n_sub = te // PEER_NKEYS
    tm = xn_ref.shape[0]
    n_up = tm // PEER_UPIECE
    n_down = o_ref.shape[1] // PEER_DPIECE
    hg = (hg0_ref, hg1_ref)
    tiles = [(c, tb) for c in range(n_sub) for tb in range(tm // LANES)]

    def body(par, up, down, first=False, last=False):
        if first:
            o_ref[...] = jnp.zeros_like(o_ref)
        todo = list(tiles) if up else []
        anchor = None
        if up:
            for k in range(n_up):
                _peer_expert_up(xn_ref, u_ref, at_ref, k)
        for k in range(n_down if down else 0):
            done = _peer_expert_down(hg[1 - par], v_ref, o_ref, k)
            left = max(0.0, 1.0 - (k + 1) / (PEER_GATE_SPAN * n_down))
            n_now = len(todo) - round(len(tiles) * left) if todo else 0
            for c, tb in todo[:n_now]:
                _peer_expert_gate(c, tb, hg[par], e1_ref, e2_ref, thr_ref, at_ref, anchor, zero_ref)
            todo = todo[n_now:]
            anchor = done
        for c, tb in todo:
            _peer_expert_gate(c, tb, hg[par], e1_ref, e2_ref, thr_ref, at_ref, None, zero_ref)
        if last:
            o_ref[...] = _rms_norm(h_ref[...] + o_ref[...], gf_ref[...])

    pl.when(e == 0)(functools.partial(body, 0, True, False, first=True))
    for par in (0, 1):
        pl.when((e >= 1) & (e < n_tiles) & (e % 2 == par))(functools.partial(body, par, True, True))
    pl.when(e == n_tiles)(functools.partial(body, n_tiles % 2, False, True, last=True))


def _peer_dense(xn2, u, v, e1, e2, thr, h1, gf, tm, te):
    rows, d = h1.shape
    n_tiles = u.shape[0] // te
    zero = jnp.zeros((SUBLANES, LANES), jnp.int32)
    return pl.pallas_call(
        functools.partial(_peer_dense_kernel, te=te, n_tiles=n_tiles),
        grid=(rows // tm, n_tiles + 1),
        in_specs=[
            pl.BlockSpec((tm, d), lambda i, e: (i, 0), pipeline_mode=pl.Buffered(1)),
            pl.BlockSpec((te, d), lambda i, e: (jnp.minimum(e, n_tiles - 1), 0)),
            pl.BlockSpec((te, d), lambda i, e: (jnp.maximum(e - 1, 0), 0)),
            pl.BlockSpec((1, PEER_HEADS, tm // LANES, te // PEER_NKEYS, LANES),
                         lambda i, e: (jnp.minimum(e, n_tiles - 1), 0, i, 0, 0)),
            pl.BlockSpec((PEER_HEADS, PEER_NKEYS, tm), lambda i, e: (0, 0, i), pipeline_mode=pl.Buffered(1)),
            pl.BlockSpec((PEER_HEADS, 1, tm), lambda i, e: (0, 0, i)),
            pl.BlockSpec((tm, d), lambda i, e: (i, 0), pipeline_mode=pl.Buffered(1)),
            pl.BlockSpec((1, d), lambda i, e: (0, 0)),
            pl.BlockSpec((SUBLANES, LANES), lambda i, e: (0, 0)),
        ],
        out_specs=pl.BlockSpec((tm, d), lambda i, e: (i, 0), pipeline_mode=pl.Buffered(1)),
        out_shape=jax.ShapeDtypeStruct((rows, d), F32),
        scratch_shapes=[pltpu.VMEM((te, tm), F32), pltpu.VMEM((tm, te), BF16), pltpu.VMEM((tm, te), BF16)],
        compiler_params=_params(("parallel", "arbitrary")),
        name="peer_dense",
    )(xn2, u, v, e1, e2, thr, h1, gf, zero)


def _tile(n, pref):
    t = min(n, pref)
    assert n % t == 0, (n, t)
    return t


def kernel(x, meta_tokens, norm1_g, w_in, w_gate_up, b_gate, gla_norm_g, w_dw, b_dw, conv_ln_g, conv_ln_b,
           w_out, norm2_g, peer_wq, peer_keys1, peer_keys2, peer_u, peer_v, final_norm_g):
    batch, seq, d = x.shape
    rows = batch * seq
    assert norm1_g.shape[0] == 1, "single-layer block"
    assert seq % CHUNK == 0 and meta_tokens.shape[0] <= CONV_HALO and (CHUNK - meta_tokens.shape[0]) >= 0
    x2 = x.reshape(rows, d)

    r_end = 2 * GLA_KEY_WIDTH + 2 * GLA_WIDTH
    gd_end = r_end + GLA_GATE_RANK
    w_in_t = w_in[0].T.astype(BF16)
    w_gd_t = jnp.pad(w_in_t[r_end:gd_end], ((0, LANES - GLA_GATE_RANK), (0, 0)))
    wgu = jnp.pad(w_gate_up[0], ((0, LANES - GLA_GATE_RANK), (0, 0))).astype(BF16)
    g1 = norm1_g[0][None]

    tm = _tile(rows, 512)
    te = 512
    n_blk = peer_u.shape[1] // CAST_ROWS
    tn_out = _tile(d, 1024)
    u_side = n_blk <= (rows // tm) * ((w_in_t.shape[0] - GLA_GATE_RANK) // 1024)
    v_side = n_blk <= (rows // tm) * (d // tn_out)
    p, gd, *u_bf = _in_proj(x2, g1, w_in_t, w_gd_t, tm, 1024, r_end, GLA_GATE_RANK, (peer_u[0],) if u_side else ())
    u_bf = u_bf[0] if u_side else peer_u[0].astype(BF16)
    pm, gdm = _in_proj(meta_tokens.astype(F32), g1, w_in_t, w_gd_t, meta_tokens.shape[0], 1024,
                       r_end, GLA_GATE_RANK)

    gla_out = _gla(p, gd, pm, gdm, wgu, b_gate[0][None], gla_norm_g[0][None], batch, _tile(seq, 512), 2)
    w_dw_p = jnp.pad(w_dw[0], ((0, CONV_HALO - CONV_KERNEL), (0, 0)))
    conv_out = _conv(p, pm, w_dw_p, b_dw[0][None], conv_ln_g[0][None], conv_ln_b[0][None], batch,
                     _tile(seq, 128))
    h1, *v_bf = _out_proj(gla_out, conv_out, w_out[0].astype(BF16), x2, tm, tn_out, (peer_v[0],) if v_side else ())
    v_bf = v_bf[0] if v_side else peer_v[0].astype(BF16)

    xn2, e1, e2, thr = _peer_score(h1, norm2_g[0][None], peer_wq[0].astype(BF16),
                                   peer_keys1[0].astype(BF16), peer_keys2[0].astype(BF16), tm, te // PEER_NKEYS)
    out = _peer_dense(xn2, u_bf, v_bf, e1, e2, thr, h1,
                      final_norm_g[None], tm, te)
    return out.reshape(batch, seq, d)
```

```python
import functools

import jax
import jax.numpy as jnp
from jax import lax
from jax.experimental import pallas as pl
from jax.experimental.pallas import tpu as pltpu

F32 = jnp.float32
BF16 = jnp.bfloat16

EPS = 1e-6
CHUNK = 64
GLA_HEADS = 8
GLA_DK = 128
GLA_DV = 256
GLA_KEY_WIDTH = GLA_HEADS * GLA_DK
GLA_WIDTH = GLA_HEADS * GLA_DV
GLA_GATE_RANK = 16
GLA_TAU = 16.0
CONV_WIDTH = 2048
CONV_KERNEL = 31
CONV_HALO = 32
PEER_HEADS = 8
PEER_NKEYS = 128
PEER_HALF = 128
PEER_TOPK = 16
PEER_UPIECE = 256
PEER_DPIECE = 256
PEER_GATE_SPAN = 1.0

LANES = 128
SUBLANES = 8
CAST_ROWS = 128
VMEM_LIMIT = 56 * 1024 * 1024

NT_DIMS = (((1,), (1,)), ((), ()))
TN_DIMS = (((0,), (0,)), ((), ()))


def _sigmoid(x):
    return 1.0 / (1.0 + jnp.exp(-x))


def _rms_norm(xf, g):
    ms = jnp.mean(xf * xf, axis=-1, keepdims=True)
    return xf * lax.rsqrt(ms + EPS) * g


def _params(sem, flags=None):
    return pltpu.CompilerParams(dimension_semantics=sem, vmem_limit_bytes=VMEM_LIMIT, flags=flags)


def _cast_specs(casts, grid):
    n_steps = 1
    for g in grid:
        n_steps *= g
    specs, first = [], 0
    for c in casts:
        n_blk = c.shape[0] // CAST_ROWS
        assert c.shape[0] % CAST_ROWS == 0 and first + n_blk <= n_steps

        def index(*ids, first=first, n_blk=n_blk):
            step = 0
            for g, i in zip(grid, ids):
                step = step * g + i
            return jnp.clip(step - first, 0, n_blk - 1), 0

        specs.append(pl.BlockSpec((CAST_ROWS, c.shape[1]), index))
        first += n_blk
    return specs


def _cast_steps(casts):
    return sum(c.shape[0] // CAST_ROWS for c in casts)


def _in_proj_kernel(x_ref, g_ref, w_ref, wgd_ref, *rest, n_cast):
    cast_in = rest[:n_cast]
    p_ref, gd_ref = rest[n_cast:n_cast + 2]
    cast_out = rest[n_cast + 2:2 * n_cast + 2]
    xn_ref = rest[2 * n_cast + 2]

    @pl.when(pl.program_id(1) == 0)
    def _():
        xn = _rms_norm(x_ref[...], g_ref[...]).astype(BF16)
        xn_ref[...] = xn
        gd_ref[...] = lax.dot_general(xn, wgd_ref[...], NT_DIMS, preferred_element_type=F32).astype(BF16)

    p_ref[...] = lax.dot_general(xn_ref[...], w_ref[...], NT_DIMS, preferred_element_type=F32).astype(BF16)
    for ci, co in zip(cast_in, cast_out):
        co[...] = ci[...].astype(BF16)


def _in_proj(x2, g, w_t, w_gd_t, tm, tn, skip_at, skip_n, casts=()):
    rows, d = x2.shape
    n = w_t.shape[0] - skip_n
    assert skip_at % tn == 0 and n % tn == 0
    n_i, n_j = rows // tm, n // tn
    cast_specs = _cast_specs(casts, (n_i, n_j))
    return pl.pallas_call(
        functools.partial(_in_proj_kernel, n_cast=len(casts)),
        grid=(n_i, n_j),
        in_specs=[
            pl.BlockSpec((tm, d), lambda i, j: (i, 0)),
            pl.BlockSpec((1, d), lambda i, j: (0, 0)),
            pl.BlockSpec((pl.Element(tn), pl.Element(d)),
                         lambda i, j: (pl.multiple_of(j * tn + jnp.where(j * tn >= skip_at, skip_n, 0), skip_n), 0)),
            pl.BlockSpec((LANES, d), lambda i, j: (0, 0)),
        ] + cast_specs,
        out_specs=[
            pl.BlockSpec((tm, tn), lambda i, j: (i, j)),
            pl.BlockSpec((tm, LANES), lambda i, j: (i, 0)),
        ] + cast_specs,
        out_shape=[
            jax.ShapeDtypeStruct((rows, n), BF16),
            jax.ShapeDtypeStruct((rows, LANES), BF16),
        ] + [jax.ShapeDtypeStruct(c.shape, BF16) for c in casts],
        scratch_shapes=[pltpu.VMEM((tm, d), BF16)],
        compiler_params=_params(("arbitrary", "arbitrary")),
        name="in_proj",
    )(x2, g, w_t, w_gd_t, *casts)


def _gla_decayed_keys(k, gd, tri, wgu, bg):
    z = jnp.dot(gd, wgu, preferred_element_type=F32) + bg
    la = (jnp.minimum(z, 0.0) - jnp.log1p(jnp.exp(-jnp.abs(z)))) * (1.0 / GLA_TAU)
    la_hi = la.astype(BF16)
    la_lo = (la - la_hi.astype(F32)).astype(BF16)
    rev = (jnp.dot(tri, la_hi, preferred_element_type=F32)
           + jnp.dot(tri, la_lo, preferred_element_type=F32))
    return (k.astype(F32) * jnp.exp(rev)).astype(BF16), la


def _gla_head_blocks(v, kdec, n_heads):
    full = lax.dot_general(v, kdec, TN_DIMS, preferred_element_type=F32)
    return [full[h * GLA_DV:(h + 1) * GLA_DV, h * GLA_DK:(h + 1) * GLA_DK] for h in range(n_heads)]


def _gla_kernel(q_ref, k_ref, v_ref, r_ref, gd_ref, km_ref, vm_ref, gdm_ref, wgu_ref, bg_ref, gn_ref, tri_ref,
                *rest, n_chunks, n_heads, n_cast):
    cast_in, o_ref, cast_out = rest[:n_cast], rest[n_cast], rest[n_cast + 1:2 * n_cast + 1]
    st_ref = rest[2 * n_cast + 1]
    for ci, co in zip(cast_in, cast_out):
        co[...] = ci[...].astype(BF16)
    wgu = wgu_ref[...]
    bg = bg_ref[...]
    n_meta = km_ref.shape[0]

    @pl.when(pl.program_id(2) == 0)
    def _():
        kdec_m, _ = _gla_decayed_keys(km_ref[...], gdm_ref[...], tri_ref[0:n_meta, 0:n_meta], wgu, bg)
        for h, blk in enumerate(_gla_head_blocks(vm_ref[...], kdec_m, n_heads)):
            st_ref[h] = blk

    kdec, la = _gla_decayed_keys(k_ref[...], gd_ref[...], tri_ref[...], wgu, bg)
    sts = [st_ref[h] for h in range(n_heads)]
    zero = jnp.zeros((GLA_DV, GLA_DK), BF16)
    for c in range(n_chunks):
        sl = slice(c * CHUNK, (c + 1) * CHUNK)
        decay = jnp.exp(jnp.sum(la[sl], axis=0, keepdims=True))
        kvs = _gla_head_blocks(v_ref[sl, :], kdec[sl], n_heads)
        rows = []
        for h in range(n_heads):
            sts[h] = sts[h] * decay[:, h * GLA_DK:(h + 1) * GLA_DK] + kvs[h]
            rows.append(jnp.concatenate([zero] * h + [sts[h].astype(BF16)] + [zero] * (n_heads - 1 - h), axis=1))
        s_blockdiag = jnp.concatenate(rows, axis=0)
        o = lax.dot_general(q_ref[sl, :], s_blockdiag, NT_DIMS,
                            preferred_element_type=F32) * (GLA_DK ** -0.5)
        on = jnp.concatenate([_rms_norm(o[:, h * GLA_DV:(h + 1) * GLA_DV], gn_ref[:, h * GLA_DV:(h + 1) * GLA_DV])
                              for h in range(n_heads)], axis=1)
        r = r_ref[sl, :].astype(F32)
        o_ref[sl, :] = (on * (r * _sigmoid(r))).astype(BF16)
    for h in range(n_heads):
        st_ref[h] = sts[h]


def _gla(p, gd, pm, gdm, wgu, bg, gn, batch, rb, hps, casts=()):
    rows = p.shape[0]
    nrb = rows // batch // rb
    kw, vw = hps * GLA_DK, hps * GLA_DV
    kb = GLA_KEY_WIDTH // kw
    vb = 2 * GLA_KEY_WIDTH // vw
    rblk = (2 * GLA_KEY_WIDTH + GLA_WIDTH) // vw
    n_meta = pm.shape[0]
    frame = jnp.arange(rb)
    tri = ((frame[:, None] // CHUNK == frame[None, :] // CHUNK) & (frame[None, :] > frame[:, None])).astype(BF16)
    row = lambda b, h, c: b * nrb + c
    grid = (batch, GLA_HEADS // hps, nrb)
    cast_specs = _cast_specs(casts, grid)
    return pl.pallas_call(
        functools.partial(_gla_kernel, n_chunks=rb // CHUNK, n_heads=hps, n_cast=len(casts)),
        grid=grid,
        in_specs=[
            pl.BlockSpec((rb, kw), lambda b, h, c: (row(b, h, c), h)),
            pl.BlockSpec((rb, kw), lambda b, h, c: (row(b, h, c), kb + h)),
            pl.BlockSpec((rb, vw), lambda b, h, c: (row(b, h, c), vb + h)),
            pl.BlockSpec((rb, vw), lambda b, h, c: (row(b, h, c), rblk + h)),
            pl.BlockSpec((rb, LANES), lambda b, h, c: (row(b, h, c), 0)),
            pl.BlockSpec((n_meta, kw), lambda b, h, c: (0, kb + h)),
            pl.BlockSpec((n_meta, vw), lambda b, h, c: (0, vb + h)),
            pl.BlockSpec((n_meta, LANES), lambda b, h, c: (0, 0)),
            pl.BlockSpec((LANES, kw), lambda b, h, c: (0, h)),
            pl.BlockSpec((1, kw), lambda b, h, c: (0, h)),
            pl.BlockSpec((1, vw), lambda b, h, c: (0, h)),
            pl.BlockSpec((rb, rb), lambda b, h, c: (0, 0)),
        ] + cast_specs,
        out_specs=[pl.BlockSpec((rb, vw), lambda b, h, c: (row(b, h, c), h))] + cast_specs,
        out_shape=[jax.ShapeDtypeStruct((rows, GLA_WIDTH), BF16)]
        + [jax.ShapeDtypeStruct(c.shape, BF16) for c in casts],
        scratch_shapes=[pltpu.VMEM((hps, GLA_DV, GLA_DK), F32)],
        compiler_params=_params(("arbitrary", "arbitrary", "arbitrary")),
        name="gla",
    )(p, p, p, p, gd, pm, pm, gdm, wgu, bg, gn, tri, *casts)


def _conv_kernel(ca_ref, cb_ref, cam_ref, cbm_ref, w_ref, b_ref, lg_ref, lb_ref, *rest, tr, n_cast):
    cast_in, o_ref, cast_out = rest[:n_cast], rest[n_cast], rest[n_cast + 1:2 * n_cast + 1]
    u_ref, y_ref = rest[2 * n_cast + 1:]
    for ci, co in zip(cast_in, cast_out):
        co[...] = ci[...].astype(BF16)
    t = pl.program_id(1)
    n_meta = cam_ref.shape[0]

    @pl.when(t == 0)
    def _():
        um = cam_ref[...].astype(F32) * _sigmoid(cbm_ref[...].astype(F32))
        u_ref[0:CONV_HALO - n_meta, :] = jnp.zeros((CONV_HALO - n_meta, CONV_WIDTH), F32)
        u_ref[CONV_HALO - n_meta:CONV_HALO, :] = um

    @pl.when(t > 0)
    def _():
        u_ref[0:CONV_HALO, :] = u_ref[tr:tr + CONV_HALO, :]

    u_ref[CONV_HALO:CONV_HALO + tr, :] = ca_ref[...].astype(F32) * _sigmoid(cb_ref[...].astype(F32))

    def strip(s, carry):
        cs = pl.ds(pl.multiple_of(s * LANES, LANES), LANES)
        x = u_ref[:, cs]
        n = CONV_HALO + tr
        acc = jnp.zeros((tr, LANES), F32)
        for sh in range(SUBLANES):
            xs = x if sh == 0 else pltpu.roll(x, n - sh, axis=0)
            for j in range(CONV_KERNEL):
                off = CONV_HALO - (CONV_KERNEL - 1) + j
                if off % SUBLANES == sh:
                    acc = acc + w_ref[j:j + 1, cs] * xs[off - sh:off - sh + tr]
        y_ref[:, cs] = acc + b_ref[:, cs]
        return carry

    lax.fori_loop(0, CONV_WIDTH // LANES, strip, 0)

    y = y_ref[...]
    mu = jnp.mean(y, axis=-1, keepdims=True)
    yc = y - mu
    var = jnp.mean(yc * yc, axis=-1, keepdims=True)
    yn = yc * lax.rsqrt(var + EPS) * lg_ref[...] + lb_ref[...]
    o_ref[...] = (yn * _sigmoid(yn)).astype(BF16)


def _conv(p, pm, w_dw, b_dw, ln_g, ln_b, batch, tr, casts=()):
    rows = p.shape[0]
    nt = rows // batch // tr
    cab = (2 * GLA_KEY_WIDTH + 2 * GLA_WIDTH) // CONV_WIDTH
    cast_specs = _cast_specs(casts, (batch, nt))
    return pl.pallas_call(
        functools.partial(_conv_kernel, tr=tr, n_cast=len(casts)),
        grid=(batch, nt),
        in_specs=[
            pl.BlockSpec((tr, CONV_WIDTH), lambda b, t: (b * nt + t, cab)),
            pl.BlockSpec((tr, CONV_WIDTH), lambda b, t: (b * nt + t, cab + 1)),
            pl.BlockSpec((pm.shape[0], CONV_WIDTH), lambda b, t: (0, cab)),
            pl.BlockSpec((pm.shape[0], CONV_WIDTH), lambda b, t: (0, cab + 1)),
            pl.BlockSpec((CONV_HALO, CONV_WIDTH), lambda b, t: (0, 0)),
            pl.BlockSpec((1, CONV_WIDTH), lambda b, t: (0, 0)),
            pl.BlockSpec((1, CONV_WIDTH), lambda b, t: (0, 0)),
            pl.BlockSpec((1, CONV_WIDTH), lambda b, t: (0, 0)),
        ] + cast_specs,
        out_specs=[pl.BlockSpec((tr, CONV_WIDTH), lambda b, t: (b * nt + t, 0))] + cast_specs,
        out_shape=[jax.ShapeDtypeStruct((rows, CONV_WIDTH), BF16)]
        + [jax.ShapeDtypeStruct(c.shape, BF16) for c in casts],
        scratch_shapes=[pltpu.VMEM((CONV_HALO + tr, CONV_WIDTH), F32), pltpu.VMEM((tr, CONV_WIDTH), F32)],
        compiler_params=_params(("arbitrary", "arbitrary")),
        name="conv",
    )(p, p, pm, pm, w_dw, b_dw, ln_g, ln_b, *casts)


def _out_proj_kernel(g_ref, c_ref, wt_ref, wb_ref, x_ref, *rest, n_cast):
    cast_in, o_ref, cast_out = rest[:n_cast], rest[n_cast], rest[n_cast + 1:]
    acc = jnp.dot(g_ref[...], wt_ref[...], preferred_element_type=F32)
    acc = acc + jnp.dot(c_ref[...], wb_ref[...], preferred_element_type=F32)
    o_ref[...] = x_ref[...] + acc
    for ci, co in zip(cast_in, cast_out):
        co[...] = ci[...].astype(BF16)


def _out_proj(gla_out, conv_out, w_out, x2, tm, tn, casts=()):
    rows, d = x2.shape
    cast_specs = _cast_specs(casts, (rows // tm, d // tn))
    return pl.pallas_call(
        functools.partial(_out_proj_kernel, n_cast=len(casts)),
        grid=(rows // tm, d // tn),
        in_specs=[
            pl.BlockSpec((tm, GLA_WIDTH), lambda i, j: (i, 0)),
            pl.BlockSpec((tm, CONV_WIDTH), lambda i, j: (i, 0)),
            pl.BlockSpec((GLA_WIDTH, tn), lambda i, j: (0, j)),
            pl.BlockSpec((CONV_WIDTH, tn), lambda i, j: (GLA_WIDTH // CONV_WIDTH, j)),
            pl.BlockSpec((tm, tn), lambda i, j: (i, j)),
        ] + cast_specs,
        out_specs=[pl.BlockSpec((tm, tn), lambda i, j: (i, j))] + cast_specs,
        out_shape=[jax.ShapeDtypeStruct((rows, d), F32)] + [jax.ShapeDtypeStruct(c.shape, BF16) for c in casts],
        compiler_params=_params(("arbitrary", "arbitrary")),
        name="out_proj",
    )(gla_out, conv_out, w_out, w_out, x2, *casts)


def _sort_network(n):
    def merge(lo, hi, r):
        step = r * 2
        if step < hi - lo:
            yield from merge(lo, hi, step)
            yield from merge(lo + r, hi, step)
            yield from ((i, i + r) for i in range(lo + r, hi - r, step))
        else:
            yield (lo, lo + r)

    def sort(lo, hi):
        if hi - lo >= 1:
            mid = lo + (hi - lo) // 2
            yield from sort(lo, mid)
            yield from sort(mid + 1, hi)
            yield from merge(lo, hi, 1)

    return tuple(sort(0, n - 1))


_KEY_SLABS = PEER_NKEYS // SUBLANES
_SLAB_SORT = _sort_network(_KEY_SLABS)


def _top_desc(s_ref, ts, n):
    slabs = [s_ref[k * SUBLANES:(k + 1) * SUBLANES, ts] for k in range(_KEY_SLABS)]
    for i, j in _SLAB_SORT:
        slabs[i], slabs[j] = jnp.maximum(slabs[i], slabs[j]), jnp.minimum(slabs[i], slabs[j])
    vals = []
    for t in range(n):
        m = jnp.max(slabs[0], axis=0, keepdims=True)
        vals.append(m)
        hit = slabs[0] == m
        for lvl in range(min(_KEY_SLABS, n - 1 - t)):
            below = slabs[lvl + 1] if lvl + 1 < _KEY_SLABS else jnp.full_like(slabs[lvl], -jnp.inf)
            slabs[lvl] = jnp.where(hit, below, slabs[lvl])
    return vals


def _top_products(p1, p2_rows, n):
    rows = p1.shape[0]
    rid = lax.broadcasted_iota(jnp.int32, (rows, LANES), 0)
    levels = []
    for b in range(n):
        nrow = rows if b == 0 else SUBLANES
        levels.append(jnp.where(rid[0:nrow] < n // (b + 1), p1[0:nrow] * p2_rows[b], -1.0))
    out = []
    for t in range(n):
        head = levels[0]
        m = jnp.max(head, axis=0, keepdims=True)
        out.append(m)
        left = n - 1 - t
        if left > 0:
            hit = head == m
            hit0 = hit[0:SUBLANES]
            levels[0] = jnp.concatenate([jnp.where(hit0, levels[1], head[0:SUBLANES]),
                                         jnp.where(hit[SUBLANES:], -1.0, head[SUBLANES:])], axis=0)
            for lvl in range(1, left):
                below = levels[lvl + 1] if lvl + 1 < n else jnp.full_like(levels[lvl], -1.0)
                levels[lvl] = jnp.where(hit0, below, levels[lvl])
    return out


def _peer_score_kernel(h_ref, g_ref, wq_ref, k1_ref, k2_ref, xn_ref, e1_ref, e2_ref, thr_ref, s1_ref, s2_ref):
    @pl.when(pl.program_id(1) == 0)
    def _():
        xn_ref[...] = _rms_norm(h_ref[...], g_ref[...]).astype(BF16)

    q = jnp.dot(xn_ref[...], wq_ref[...], preferred_element_type=F32).astype(BF16)
    s1_ref[...] = lax.dot_general(k1_ref[0], q[:, :PEER_HALF], NT_DIMS, preferred_element_type=F32)
    s2_ref[...] = lax.dot_general(k2_ref[0], q[:, PEER_HALF:], NT_DIMS, preferred_element_type=F32)
    n = PEER_TOPK + 1
    pad = 3 * SUBLANES - n
    for tb in range(s1_ref.shape[1] // LANES):
        ts = slice(tb * LANES, (tb + 1) * LANES)
        v1 = _top_desc(s1_ref, ts, n)
        v2 = _top_desc(s2_ref, ts, n)
        p1 = jnp.exp(jnp.concatenate(v1 + [v1[-1]] * pad, axis=0) - v1[0])
        p2 = jnp.exp(jnp.concatenate(v2 + [v2[-1]] * pad, axis=0) - v2[0])
        w = _top_products(p1, [p2[b:b + 1] for b in range(n)], n)
        z = w[0]
        for t in range(1, PEER_TOPK):
            z = z + w[t]
        rz = 1.0 / z
        e1n = jnp.exp(s1_ref[:, ts] - v1[0]) * rz
        grp = e1_ref.shape[3]
        for kb in range(PEER_NKEYS // grp):
            e1_ref[kb, 0, tb] = e1n[kb * grp:(kb + 1) * grp]
        e2_ref[0, :, ts] = jnp.exp(s2_ref[:, ts] - v2[0])
        thr_ref[0, :, ts] = 0.5 * (w[PEER_TOPK - 1] + w[PEER_TOPK]) * rz


def _peer_score(h1, g2, wq, k1, k2, tm, grp):
    rows, d = h1.shape
    qw = 2 * PEER_HALF
    return pl.pallas_call(
        _peer_score_kernel,
        grid=(rows // tm, PEER_HEADS),
        in_specs=[
            pl.BlockSpec((tm, d), lambda i, h: (i, 0)),
            pl.BlockSpec((1, d), lambda i, h: (0, 0)),
            pl.BlockSpec((d, qw), lambda i, h: (0, h)),
            pl.BlockSpec((1, PEER_NKEYS, PEER_HALF), lambda i, h: (h, 0, 0)),
            pl.BlockSpec((1, PEER_NKEYS, PEER_HALF), lambda i, h: (h, 0, 0)),
        ],
        out_specs=[
            pl.BlockSpec((tm, d), lambda i, h: (i, 0)),
            pl.BlockSpec((PEER_NKEYS // grp, 1, tm // LANES, grp, LANES), lambda i, h: (0, h, i, 0, 0)),
            pl.BlockSpec((1, PEER_NKEYS, tm), lambda i, h: (h, 0, i)),
            pl.BlockSpec((1, 1, tm), lambda i, h: (h, 0, i)),
        ],
        out_shape=[
            jax.ShapeDtypeStruct((rows, d), BF16),
            jax.ShapeDtypeStruct((PEER_NKEYS // grp, PEER_HEADS, rows // LANES, grp, LANES), F32),
            jax.ShapeDtypeStruct((PEER_HEADS, PEER_NKEYS, rows), F32),
            jax.ShapeDtypeStruct((PEER_HEADS, 1, rows), F32),
        ],
        scratch_shapes=[pltpu.VMEM((PEER_NKEYS, tm), F32), pltpu.VMEM((PEER_NKEYS, tm), F32)],
        compiler_params=_params(("parallel", "arbitrary")),
        name="peer_score",
    )(h1, g2, wq, k1, k2)


def _peer_expert_up(xn_ref, u_ref, at_ref, piece):
    ts = slice(piece * PEER_UPIECE, (piece + 1) * PEER_UPIECE)
    r = lax.dot_general(u_ref[...], xn_ref[ts, :], NT_DIMS, preferred_element_type=F32)
    at_ref[:, ts] = r
    return r[r.shape[0] - SUBLANES:, PEER_UPIECE - LANES:]


def _after(row, anchor, zero_ref):
    z = pltpu.bitcast(anchor, jnp.int32) & zero_ref[...]
    return row + pltpu.bitcast(z, F32)[0:1, :]


def _peer_expert_gate(c, tb, hg_cur, e1_ref, e2_ref, thr_ref, at_ref, anchor, zero_ref):
    rs = slice(c * PEER_NKEYS, (c + 1) * PEER_NKEYS)
    ts = slice(tb * LANES, (tb + 1) * LANES)
    g = jnp.zeros((PEER_NKEYS, LANES), F32)
    for h in range(PEER_HEADS):
        e1_row = e1_ref[0, h, tb, c:c + 1, :]
        if anchor is not None:
            e1_row = _after(e1_row, anchor, zero_ref)
        w = e1_row * e2_ref[h, :, ts]
        g = g + jnp.where(w >= thr_ref[h, :, ts], w, 0.0)
    a = at_ref[rs, ts]
    gelu = 0.5 * a * (1.0 + lax.erf(a * (2.0 ** -0.5)))
    hg_cur[ts, rs] = (gelu * g).T.astype(BF16)


def _peer_expert_down(hg_prev, v_ref, o_ref, piece):
    cs = slice(piece * PEER_DPIECE, (piece + 1) * PEER_DPIECE)
    r = jnp.dot(hg_prev[...], v_ref[:, cs], preferred_element_type=F32)
    o_ref[:, cs] += r
    return r[r.shape[0] - SUBLANES:, PEER_DPIECE - LANES:]


def _peer_dense_kernel(xn_ref, u_ref, v_ref, e1_ref, e2_ref, thr_ref, h_ref, gf_ref, zero_ref, o_ref,
                       at_ref, hg0_ref, hg1_ref, *, te, n_tiles):
    e = pl.program_id(1)
    n_sub = te // PEER_NKEYS
    tm = xn_ref.shape[0]
    n_up = tm // PEER_UPIECE
    n_down = o_ref.shape[1] // PEER_DPIECE
    hg = (hg0_ref, hg1_ref)
    tiles = [(c, tb) for c in range(n_sub) for tb in range(tm // LANES)]

    def body(par, up, down, first=False, last=False):
        if first:
            o_ref[...] = jnp.zeros_like(o_ref)
        todo = list(tiles) if up else []
        anchor = None
        if up:
            for k in range(n_up):
                _peer_expert_up(xn_ref, u_ref, at_ref, k)
        for k in range(n_down if down else 0):
            done = _peer_expert_down(hg[1 - par], v_ref, o_ref, k)
            left = max(0.0, 1.0 - (k + 1) / (PEER_GATE_SPAN * n_down))
            n_now = len(todo) - round(len(tiles) * left) if todo else 0
            for c, tb in todo[:n_now]:
                _peer_expert_gate(c, tb, hg[par], e1_ref, e2_ref, thr_ref, at_ref, anchor, zero_ref)
            todo = todo[n_now:]
            anchor = done
        for c, tb in todo:
            _peer_expert_gate(c, tb, hg[par], e1_ref, e2_ref, thr_ref, at_ref, None, zero_ref)
        if last:
            o_ref[...] = _rms_norm(h_ref[...] + o_ref[...], gf_ref[...])

    pl.when(e == 0)(functools.partial(body, 0, True, False, first=True))
    for par in (0, 1):
        pl.when((e >= 1) & (e < n_tiles) & (e % 2 == par))(functools.partial(body, par, True, True))
    pl.when(e == n_tiles)(functools.partial(body, n_tiles % 2, False, True, last=True))


def _peer_dense(xn2, u, v, e1, e2, thr, h1, gf, tm, te):
    rows, d = h1.shape
    n_tiles = u.shape[0] // te
    zero = jnp.zeros((SUBLANES, LANES), jnp.int32)
    return pl.pallas_call(
        functools.partial(_peer_dense_kernel, te=te, n_tiles=n_tiles),
        grid=(rows // tm, n_tiles + 1),
        in_specs=[
            pl.BlockSpec((tm, d), lambda i, e: (i, 0), pipeline_mode=pl.Buffered(1)),
            pl.BlockSpec((te, d), lambda i, e: (jnp.minimum(e, n_tiles - 1), 0)),
            pl.BlockSpec((te, d), lambda i, e: (jnp.maximum(e - 1, 0), 0)),
            pl.BlockSpec((1, PEER_HEADS, tm // LANES, te // PEER_NKEYS, LANES),
                         lambda i, e: (jnp.minimum(e, n_tiles - 1), 0, i, 0, 0)),
            pl.BlockSpec((PEER_HEADS, PEER_NKEYS, tm), lambda i, e: (0, 0, i), pipeline_mode=pl.Buffered(1)),
            pl.BlockSpec((PEER_HEADS, 1, tm), lambda i, e: (0, 0, i)),
            pl.BlockSpec((tm, d), lambda i, e: (i, 0), pipeline_mode=pl.Buffered(1)),
            pl.BlockSpec((1, d), lambda i, e: (0, 0)),
            pl.BlockSpec((SUBLANES, LANES), lambda i, e: (0, 0)),
        ],
        out_specs=pl.BlockSpec((tm, d), lambda i, e: (i, 0), pipeline_mode=pl.Buffered(1)),
        out_shape=jax.ShapeDtypeStruct((rows, d), F32),
        scratch_shapes=[pltpu.VMEM((te, tm), F32), pltpu.VMEM((tm, te), BF16), pltpu.VMEM((tm, te), BF16)],
        compiler_params=_params(("parallel", "arbitrary")),
        name="peer_dense",
    )(xn2, u, v, e1, e2, thr, h1, gf, zero)


def _tile(n, pref):
    t = min(n, pref)
    assert n % t == 0, (n, t)
    return t


def kernel(x, meta_tokens, norm1_g, w_in, w_gate_up, b_gate, gla_norm_g, w_dw, b_dw, conv_ln_g, conv_ln_b,
           w_out, norm2_g, peer_wq, peer_keys1, peer_keys2, peer_u, peer_v, final_norm_g):
    batch, seq, d = x.shape
    rows = batch * seq
    assert norm1_g.shape[0] == 1, "single-layer block"
    assert seq % CHUNK == 0 and meta_tokens.shape[0] <= CONV_HALO and (CHUNK - meta_tokens.shape[0]) >= 0
    x2 = x.reshape(rows, d)

    r_end = 2 * GLA_KEY_WIDTH + 2 * GLA_WIDTH
    gd_end = r_end + GLA_GATE_RANK
    w_in_t = w_in[0].T.astype(BF16)
    w_gd_t = jnp.pad(w_in_t[r_end:gd_end], ((0, LANES - GLA_GATE_RANK), (0, 0)))
    wgu = jnp.pad(w_gate_up[0], ((0, LANES - GLA_GATE_RANK), (0, 0))).astype(BF16)
    g1 = norm1_g[0][None]

    tm = _tile(rows, 512)
    te = 512
    rb, tr, hps = _tile(seq, 512), _tile(seq, 128), 2
    in_steps = (rows // tm) * ((w_in_t.shape[0] - GLA_GATE_RANK) // 1024)
    side_w = _cast_steps((w_out[0], peer_wq[0])) <= in_steps
    side_u = _cast_steps((peer_u[0],)) <= batch * (GLA_HEADS // hps) * (seq // rb)
    side_v = _cast_steps((peer_v[0],)) <= batch * (seq // tr)

    p, gd, *w_bf = _in_proj(x2, g1, w_in_t, w_gd_t, tm, 1024, r_end, GLA_GATE_RANK,
                            (w_out[0], peer_wq[0]) if side_w else ())
    w_out_bf, wq_bf = w_bf if side_w else (w_out[0].astype(BF16), peer_wq[0].astype(BF16))
    pm, gdm = _in_proj(meta_tokens.astype(F32), g1, w_in_t, w_gd_t, meta_tokens.shape[0], 1024,
                       r_end, GLA_GATE_RANK)

    gla_out, *u_bf = _gla(p, gd, pm, gdm, wgu, b_gate[0][None], gla_norm_g[0][None], batch, rb, hps,
                          (peer_u[0],) if side_u else ())
    u_bf = u_bf[0] if side_u else peer_u[0].astype(BF16)
    w_dw_p = jnp.pad(w_dw[0], ((0, CONV_HALO - CONV_KERNEL), (0, 0)))
    conv_out, *v_bf = _conv(p, pm, w_dw_p, b_dw[0][None], conv_ln_g[0][None], conv_ln_b[0][None], batch, tr,
                            (peer_v[0],) if side_v else ())
    v_bf = v_bf[0] if side_v else peer_v[0].astype(BF16)
    h1, = _out_proj(gla_out, conv_out, w_out_bf, x2, tm, _tile(d, 1024))

    xn2, e1, e2, thr = _peer_score(h1, norm2_g[0][None], wq_bf,
                                   peer_keys1[0].astype(BF16), peer_keys2[0].astype(BF16), tm, te // PEER_NKEYS)
    out = _peer_dense(xn2, u_bf, v_bf, e1, e2, thr, h1,
                      final_norm_g[None], tm, te)
    return out.reshape(batch, seq, d)
```

```python
import functools

import jax
import jax.numpy as jnp
from jax import lax
from jax.experimental import pallas as pl
from jax.experimental.pallas import tpu as pltpu

F32 = jnp.float32
BF16 = jnp.bfloat16

EPS = 1e-6
CHUNK = 64
GLA_HEADS = 8
GLA_DK = 128
GLA_DV = 256
GLA_KEY_WIDTH = GLA_HEADS * GLA_DK
GLA_WIDTH = GLA_HEADS * GLA_DV
GLA_GATE_RANK = 16
GLA_TAU = 16.0
CONV_WIDTH = 2048
CONV_KERNEL = 31
CONV_HALO = 32
PEER_HEADS = 8
PEER_NKEYS = 128
PEER_HALF = 128
PEER_TOPK = 16
PEER_UPIECE = 256
PEER_DPIECE = 256
PEER_GATE_SPAN = 1.0

LANES = 128
SUBLANES = 8
CAST_ROWS = 128
VMEM_LIMIT = 56 * 1024 * 1024

NT_DIMS = (((1,), (1,)), ((), ()))
TN_DIMS = (((0,), (0,)), ((), ()))


def _sigmoid(x):
    return 1.0 / (1.0 + jnp.exp(-x))


def _rms_norm(xf, g):
    ms = jnp.mean(xf * xf, axis=-1, keepdims=True)
    return xf * lax.rsqrt(ms + EPS) * g


def _params(sem, flags=None):
    return pltpu.CompilerParams(dimension_semantics=sem, vmem_limit_bytes=VMEM_LIMIT, flags=flags)


def _cast_specs(casts, grid):
    n_steps = 1
    for g in grid:
        n_steps *= g
    specs, first = [], 0
    for c in casts:
        n_blk = c.shape[0] // CAST_ROWS
        assert c.shape[0] % CAST_ROWS == 0 and first + n_blk <= n_steps

        def index(*ids, first=first, n_blk=n_blk):
            step = 0
            for g, i in zip(grid, ids):
                step = step * g + i
            return jnp.clip(step - first, 0, n_blk - 1), 0

        specs.append(pl.BlockSpec((CAST_ROWS, c.shape[1]), index))
        first += n_blk
    return specs


def _cast_steps(casts):
    return sum(c.shape[0] // CAST_ROWS for c in casts)


def _in_proj_kernel(x_ref, g_ref, w_ref, wgd_ref, *rest, n_cast):
    cast_in = rest[:n_cast]
    p_ref, gd_ref = rest[n_cast:n_cast + 2]
    cast_out = rest[n_cast + 2:2 * n_cast + 2]
    xn_ref = rest[2 * n_cast + 2]

    @pl.when(pl.program_id(1) == 0)
    def _():
        xn = _rms_norm(x_ref[...], g_ref[...]).astype(BF16)
        xn_ref[...] = xn
        gd_ref[...] = lax.dot_general(xn, wgd_ref[...], NT_DIMS, preferred_element_type=F32).astype(BF16)

    p_ref[...] = lax.dot_general(xn_ref[...], w_ref[...], NT_DIMS, preferred_element_type=F32).astype(BF16)
    for ci, co in zip(cast_in, cast_out):
        co[...] = ci[...].astype(BF16)


def _in_proj(x2, g, w_t, w_gd_t, tm, tn, skip_at, skip_n, casts=()):
    rows, d = x2.shape
    n = w_t.shape[0] - skip_n
    assert skip_at % tn == 0 and n % tn == 0
    n_i, n_j = rows // tm, n // tn
    cast_specs = _cast_specs(casts, (n_i, n_j))
    return pl.pallas_call(
        functools.partial(_in_proj_kernel, n_cast=len(casts)),
        grid=(n_i, n_j),
        in_specs=[
            pl.BlockSpec((tm, d), lambda i, j: (i, 0)),
            pl.BlockSpec((1, d), lambda i, j: (0, 0)),
            pl.BlockSpec((pl.Element(tn), pl.Element(d)),
                         lambda i, j: (pl.multiple_of(j * tn + jnp.where(j * tn >= skip_at, skip_n, 0), skip_n), 0)),
            pl.BlockSpec((LANES, d), lambda i, j: (0, 0)),
        ] + cast_specs,
        out_specs=[
            pl.BlockSpec((tm, tn), lambda i, j: (i, j)),
            pl.BlockSpec((tm, LANES), lambda i, j: (i, 0)),
        ] + cast_specs,
        out_shape=[
            jax.ShapeDtypeStruct((rows, n), BF16),
            jax.ShapeDtypeStruct((rows, LANES), BF16),
        ] + [jax.ShapeDtypeStruct(c.shape, BF16) for c in casts],
        scratch_shapes=[pltpu.VMEM((tm, d), BF16)],
        compiler_params=_params(("arbitrary", "arbitrary")),
        name="in_proj",
    )(x2, g, w_t, w_gd_t, *casts)


def _gla_decayed_keys(k, gd, tri, wgu, bg):
    z = jnp.dot(gd, wgu, preferred_element_type=F32) + bg
    la = (jnp.minimum(z, 0.0) - jnp.log1p(jnp.exp(-jnp.abs(z)))) * (1.0 / GLA_TAU)
    la_hi = la.astype(BF16)
    la_lo = (la - la_hi.astype(F32)).astype(BF16)
    rev = (jnp.dot(tri, la_hi, preferred_element_type=F32)
           + jnp.dot(tri, la_lo, preferred_element_type=F32))
    return (k.astype(F32) * jnp.exp(rev)).astype(BF16), la


def _gla_head_blocks(v, kdec, n_heads):
    full = lax.dot_general(v, kdec, TN_DIMS, preferred_element_type=F32)
    return [full[h * GLA_DV:(h + 1) * GLA_DV, h * GLA_DK:(h + 1) * GLA_DK] for h in range(n_heads)]


def _gla_kernel(q_ref, k_ref, v_ref, r_ref, gd_ref, km_ref, vm_ref, gdm_ref, wgu_ref, bg_ref, gn_ref, tri_ref,
                *rest, n_chunks, n_heads, n_cast):
    cast_in, o_ref, cast_out = rest[:n_cast], rest[n_cast], rest[n_cast + 1:2 * n_cast + 1]
    st_ref = rest[2 * n_cast + 1]
    for ci, co in zip(cast_in, cast_out):
        co[...] = ci[...].astype(BF16)
    wgu = wgu_ref[...]
    bg = bg_ref[...]
    n_meta = km_ref.shape[0]

    @pl.when(pl.program_id(2) == 0)
    def _():
        kdec_m, _ = _gla_decayed_keys(km_ref[...], gdm_ref[...], tri_ref[0:n_meta, 0:n_meta], wgu, bg)
        for h, blk in enumerate(_gla_head_blocks(vm_ref[...], kdec_m, n_heads)):
            st_ref[h] = blk

    kdec, la = _gla_decayed_keys(k_ref[...], gd_ref[...], tri_ref[...], wgu, bg)
    sts = [st_ref[h] for h in range(n_heads)]
    zero = jnp.zeros((GLA_DV, GLA_DK), BF16)
    for c in range(n_chunks):
        sl = slice(c * CHUNK, (c + 1) * CHUNK)
        decay = jnp.exp(jnp.sum(la[sl], axis=0, keepdims=True))
        kvs = _gla_head_blocks(v_ref[sl, :], kdec[sl], n_heads)
        rows = []
        for h in range(n_heads):
            sts[h] = sts[h] * decay[:, h * GLA_DK:(h + 1) * GLA_DK] + kvs[h]
            rows.append(jnp.concatenate([zero] * h + [sts[h].astype(BF16)] + [zero] * (n_heads - 1 - h), axis=1))
        s_blockdiag = jnp.concatenate(rows, axis=0)
        o = lax.dot_general(q_ref[sl, :], s_blockdiag, NT_DIMS,
                            preferred_element_type=F32) * (GLA_DK ** -0.5)
        on = jnp.concatenate([_rms_norm(o[:, h * GLA_DV:(h + 1) * GLA_DV], gn_ref[:, h * GLA_DV:(h + 1) * GLA_DV])
                              for h in range(n_heads)], axis=1)
        r = r_ref[sl, :].astype(F32)
        o_ref[sl, :] = (on * (r * _sigmoid(r))).astype(BF16)
    for h in range(n_heads):
        st_ref[h] = sts[h]


def _gla(p, gd, pm, gdm, wgu, bg, gn, batch, rb, hps, casts=()):
    rows = p.shape[0]
    nrb = rows // batch // rb
    kw, vw = hps * GLA_DK, hps * GLA_DV
    kb = GLA_KEY_WIDTH // kw
    vb = 2 * GLA_KEY_WIDTH // vw
    rblk = (2 * GLA_KEY_WIDTH + GLA_WIDTH) // vw
    n_meta = pm.shape[0]
    frame = jnp.arange(rb)
    tri = ((frame[:, None] // CHUNK == frame[None, :] // CHUNK) & (frame[None, :] > frame[:, None])).astype(BF16)
    row = lambda b, h, c: b * nrb + c
    grid = (batch, GLA_HEADS // hps, nrb)
    cast_specs = _cast_specs(casts, grid)
    return pl.pallas_call(
        functools.partial(_gla_kernel, n_chunks=rb // CHUNK, n_heads=hps, n_cast=len(casts)),
        grid=grid,
        in_specs=[
            pl.BlockSpec((rb, kw), lambda b, h, c: (row(b, h, c), h)),
            pl.BlockSpec((rb, kw), lambda b, h, c: (row(b, h, c), kb + h)),
            pl.BlockSpec((rb, vw), lambda b, h, c: (row(b, h, c), vb + h)),
            pl.BlockSpec((rb, vw), lambda b, h, c: (row(b, h, c), rblk + h)),
            pl.BlockSpec((rb, LANES), lambda b, h, c: (row(b, h, c), 0)),
            pl.BlockSpec((n_meta, kw), lambda b, h, c: (0, kb + h)),
            pl.BlockSpec((n_meta, vw), lambda b, h, c: (0, vb + h)),
            pl.BlockSpec((n_meta, LANES), lambda b, h, c: (0, 0)),
            pl.BlockSpec((LANES, kw), lambda b, h, c: (0, h)),
            pl.BlockSpec((1, kw), lambda b, h, c: (0, h)),
            pl.BlockSpec((1, vw), lambda b, h, c: (0, h)),
            pl.BlockSpec((rb, rb), lambda b, h, c: (0, 0)),
        ] + cast_specs,
        out_specs=[pl.BlockSpec((rb, vw), lambda b, h, c: (row(b, h, c), h))] + cast_specs,
        out_shape=[jax.ShapeDtypeStruct((rows, GLA_WIDTH), BF16)]
        + [jax.ShapeDtypeStruct(c.shape, BF16) for c in casts],
        scratch_shapes=[pltpu.VMEM((hps, GLA_DV, GLA_DK), F32)],
        compiler_params=_params(("arbitrary", "arbitrary", "arbitrary")),
        name="gla",
    )(p, p, p, p, gd, pm, pm, gdm, wgu, bg, gn, tri, *casts)


def _conv_kernel(ca_ref, cb_ref, cam_ref, cbm_ref, w_ref, b_ref, lg_ref, lb_ref, *rest, tr, n_cast):
    cast_in, o_ref, cast_out = rest[:n_cast], rest[n_cast], rest[n_cast + 1:2 * n_cast + 1]
    u_ref, y_ref = rest[2 * n_cast + 1:]
    for ci, co in zip(cast_in, cast_out):
        co[...] = ci[...].astype(BF16)
    t = pl.program_id(1)
    n_meta = cam_ref.shape[0]

    @pl.when(t == 0)
    def _():
        um = cam_ref[...].astype(F32) * _sigmoid(cbm_ref[...].astype(F32))
        u_ref[0:CONV_HALO - n_meta, :] = jnp.zeros((CONV_HALO - n_meta, CONV_WIDTH), F32)
        u_ref[CONV_HALO - n_meta:CONV_HALO, :] = um

    @pl.when(t > 0)
    def _():
        u_ref[0:CONV_HALO, :] = u_ref[tr:tr + CONV_HALO, :]

    u_ref[CONV_HALO:CONV_HALO + tr, :] = ca_ref[...].astype(F32) * _sigmoid(cb_ref[...].astype(F32))

    def strip(s, carry):
        cs = pl.ds(pl.multiple_of(s * LANES, LANES), LANES)
        x = u_ref[:, cs]
        n = CONV_HALO + tr
        acc = jnp.zeros((tr, LANES), F32)
        for sh in range(SUBLANES):
            xs = x if sh == 0 else pltpu.roll(x, n - sh, axis=0)
            for j in range(CONV_KERNEL):
                off = CONV_HALO - (CONV_KERNEL - 1) + j
                if off % SUBLANES == sh:
                    acc = acc + w_ref[j:j + 1, cs] * xs[off - sh:off - sh + tr]
        y_ref[:, cs] = acc + b_ref[:, cs]
        return carry

    lax.fori_loop(0, CONV_WIDTH // LANES, strip, 0)

    y = y_ref[...]
    mu = jnp.mean(y, axis=-1, keepdims=True)
    yc = y - mu
    var = jnp.mean(yc * yc, axis=-1, keepdims=True)
    yn = yc * lax.rsqrt(var + EPS) * lg_ref[...] + lb_ref[...]
    o_ref[...] = (yn * _sigmoid(yn)).astype(BF16)


def _conv(p, pm, w_dw, b_dw, ln_g, ln_b, batch, tr, casts=()):
    rows = p.shape[0]
    nt = rows // batch // tr
    cab = (2 * GLA_KEY_WIDTH + 2 * GLA_WIDTH) // CONV_WIDTH
    cast_specs = _cast_specs(casts, (batch, nt))
    return pl.pallas_call(
        functools.partial(_conv_kernel, tr=tr, n_cast=len(casts)),
        grid=(batch, nt),
        in_specs=[
            pl.BlockSpec((tr, CONV_WIDTH), lambda b, t: (b * nt + t, cab)),
            pl.BlockSpec((tr, CONV_WIDTH), lambda b, t: (b * nt + t, cab + 1)),
            pl.BlockSpec((pm.shape[0], CONV_WIDTH), lambda b, t: (0, cab)),
            pl.BlockSpec((pm.shape[0], CONV_WIDTH), lambda b, t: (0, cab + 1)),
            pl.BlockSpec((CONV_HALO, CONV_WIDTH), lambda b, t: (0, 0)),
            pl.BlockSpec((1, CONV_WIDTH), lambda b, t: (0, 0)),
            pl.BlockSpec((1, CONV_WIDTH), lambda b, t: (0, 0)),
            pl.BlockSpec((1, CONV_WIDTH), lambda b, t: (0, 0)),
        ] + cast_specs,
        out_specs=[pl.BlockSpec((tr, CONV_WIDTH), lambda b, t: (b * nt + t, 0))] + cast_specs,
        out_shape=[jax.ShapeDtypeStruct((rows, CONV_WIDTH), BF16)]
        + [jax.ShapeDtypeStruct(c.shape, BF16) for c in casts],
        scratch_shapes=[pltpu.VMEM((CONV_HALO + tr, CONV_WIDTH), F32), pltpu.VMEM((tr, CONV_WIDTH), F32)],
        compiler_params=_params(("arbitrary", "arbitrary")),
        name="conv",
    )(p, p, pm, pm, w_dw, b_dw, ln_g, ln_b, *casts)


def _out_proj_kernel(g_ref, c_ref, wt_ref, wb_ref, x_ref, *rest, n_cast):
    cast_in, o_ref, cast_out = rest[:n_cast], rest[n_cast], rest[n_cast + 1:]
    acc = jnp.dot(g_ref[...], wt_ref[...], preferred_element_type=F32)
    acc = acc + jnp.dot(c_ref[...], wb_ref[...], preferred_element_type=F32)
    o_ref[...] = x_ref[...] + acc
    for ci, co in zip(cast_in, cast_out):
        co[...] = ci[...].astype(BF16)


def _out_proj(gla_out, conv_out, w_out, x2, tm, tn, casts=()):
    rows, d = x2.shape
    cast_specs = _cast_specs(casts, (rows // tm, d // tn))
    return pl.pallas_call(
        functools.partial(_out_proj_kernel, n_cast=len(casts)),
        grid=(rows // tm, d // tn),
        in_specs=[
            pl.BlockSpec((tm, GLA_WIDTH), lambda i, j: (i, 0)),
            pl.BlockSpec((tm, CONV_WIDTH), lambda i, j: (i, 0)),
            pl.BlockSpec((GLA_WIDTH, tn), lambda i, j: (0, j)),
            pl.BlockSpec((CONV_WIDTH, tn), lambda i, j: (GLA_WIDTH // CONV_WIDTH, j)),
            pl.BlockSpec((tm, tn), lambda i, j: (i, j)),
        ] + cast_specs,
        out_specs=[pl.BlockSpec((tm, tn), lambda i, j: (i, j))] + cast_specs,
        out_shape=[jax.ShapeDtypeStruct((rows, d), F32)] + [jax.ShapeDtypeStruct(c.shape, BF16) for c in casts],
        compiler_params=_params(("arbitrary", "arbitrary")),
        name="out_proj",
    )(gla_out, conv_out, w_out, w_out, x2, *casts)


def _sort_network(n):
    def merge(lo, hi, r):
        step = r * 2
        if step < hi - lo:
            yield from merge(lo, hi, step)
            yield from merge(lo + r, hi, step)
            yield from ((i, i + r) for i in range(lo + r, hi - r, step))
        else:
            yield (lo, lo + r)

    def sort(lo, hi):
        if hi - lo >= 1:
            mid = lo + (hi - lo) // 2
            yield from sort(lo, mid)
            yield from sort(mid + 1, hi)
            yield from merge(lo, hi, 1)

    return tuple(sort(0, n - 1))


_KEY_SLABS = PEER_NKEYS // SUBLANES
_SLAB_SORT = _sort_network(_KEY_SLABS)


def _top_desc(s_ref, ts, n):
    slabs = [s_ref[k * SUBLANES:(k + 1) * SUBLANES, ts] for k in range(_KEY_SLABS)]
    for i, j in _SLAB_SORT:
        slabs[i], slabs[j] = jnp.maximum(slabs[i], slabs[j]), jnp.minimum(slabs[i], slabs[j])
    vals = []
    for t in range(n):
        m = jnp.max(slabs[0], axis=0, keepdims=True)
        vals.append(m)
        hit = slabs[0] == m
        for lvl in range(min(_KEY_SLABS, n - 1 - t)):
            below = slabs[lvl + 1] if lvl + 1 < _KEY_SLABS else jnp.full_like(slabs[lvl], -jnp.inf)
            slabs[lvl] = jnp.where(hit, below, slabs[lvl])
    return vals


def _top_products(p1, p2_rows, n):
    rows = p1.shape[0]
    rid = lax.broadcasted_iota(jnp.int32, (rows, LANES), 0)
    levels = []
    for b in range(n):
        nrow = rows if b == 0 else SUBLANES
        levels.append(jnp.where(rid[0:nrow] < n // (b + 1), p1[0:nrow] * p2_rows[b], -1.0))
    out = []
    for t in range(n):
        head = levels[0]
        m = jnp.max(head, axis=0, keepdims=True)
        out.append(m)
        left = n - 1 - t
        if left > 0:
            hit = head == m
            hit0 = hit[0:SUBLANES]
            levels[0] = jnp.concatenate([jnp.where(hit0, levels[1], head[0:SUBLANES]),
                                         jnp.where(hit[SUBLANES:], -1.0, head[SUBLANES:])], axis=0)
            for lvl in range(1, left):
                below = levels[lvl + 1] if lvl + 1 < n else jnp.full_like(levels[lvl], -1.0)
                levels[lvl] = jnp.where(hit0, below, levels[lvl])
    return out


def _peer_score_kernel(h_ref, g_ref, wq_ref, k1_ref, k2_ref, xn_ref, e1_ref, e2_ref, thr_ref, s1_ref, s2_ref):
    @pl.when(pl.program_id(1) == 0)
    def _():
        xn_ref[...] = _rms_norm(h_ref[...], g_ref[...]).astype(BF16)

    q = jnp.dot(xn_ref[...], wq_ref[...], preferred_element_type=F32).astype(BF16)
    s1_ref[...] = lax.dot_general(k1_ref[0], q[:, :PEER_HALF], NT_DIMS, preferred_element_type=F32)
    s2_ref[...] = lax.dot_general(k2_ref[0], q[:, PEER_HALF:], NT_DIMS, preferred_element_type=F32)
    n = PEER_TOPK + 1
    pad = 3 * SUBLANES - n
    for tb in range(s1_ref.shape[1] // LANES):
        ts = slice(tb * LANES, (tb + 1) * LANES)
        v1 = _top_desc(s1_ref, ts, n)
        v2 = _top_desc(s2_ref, ts, n)
        p1 = jnp.exp(jnp.concatenate(v1 + [v1[-1]] * pad, axis=0) - v1[0])
        p2 = jnp.exp(jnp.concatenate(v2 + [v2[-1]] * pad, axis=0) - v2[0])
        w = _top_products(p1, [p2[b:b + 1] for b in range(n)], n)
        z = w[0]
        for t in range(1, PEER_TOPK):
            z = z + w[t]
        rz = 1.0 / z
        e1n = jnp.exp(s1_ref[:, ts] - v1[0]) * rz
        grp = e1_ref.shape[3]
        for kb in range(PEER_NKEYS // grp):
            e1_ref[kb, 0, tb] = e1n[kb * grp:(kb + 1) * grp]
        e2_ref[0, :, ts] = jnp.exp(s2_ref[:, ts] - v2[0])
        thr_ref[0, :, ts] = 0.5 * (w[PEER_TOPK - 1] + w[PEER_TOPK]) * rz


def _peer_score(h1, g2, wq, k1, k2, tm, grp):
    rows, d = h1.shape
    qw = 2 * PEER_HALF
    return pl.pallas_call(
        _peer_score_kernel,
        grid=(rows // tm, PEER_HEADS),
        in_specs=[
            pl.BlockSpec((tm, d), lambda i, h: (i, 0)),
            pl.BlockSpec((1, d), lambda i, h: (0, 0)),
            pl.BlockSpec((d, qw), lambda i, h: (0, h)),
            pl.BlockSpec((1, PEER_NKEYS, PEER_HALF), lambda i, h: (h, 0, 0)),
            pl.BlockSpec((1, PEER_NKEYS, PEER_HALF), lambda i, h: (h, 0, 0)),
        ],
        out_specs=[
            pl.BlockSpec((tm, d), lambda i, h: (i, 0)),
            pl.BlockSpec((PEER_NKEYS // grp, 1, tm // LANES, grp, LANES), lambda i, h: (0, h, i, 0, 0)),
            pl.BlockSpec((1, PEER_NKEYS, tm), lambda i, h: (h, 0, i)),
            pl.BlockSpec((1, 1, tm), lambda i, h: (h, 0, i)),
        ],
        out_shape=[
            jax.ShapeDtypeStruct((rows, d), BF16),
            jax.ShapeDtypeStruct((PEER_NKEYS // grp, PEER_HEADS, rows // LANES, grp, LANES), F32),
            jax.ShapeDtypeStruct((PEER_HEADS, PEER_NKEYS, rows), F32),
            jax.ShapeDtypeStruct((PEER_HEADS, 1, rows), F32),
        ],
        scratch_shapes=[pltpu.VMEM((PEER_NKEYS, tm), F32), pltpu.VMEM((PEER_NKEYS, tm), F32)],
        compiler_params=_params(("parallel", "arbitrary")),
        name="peer_score",
    )(h1, g2, wq, k1, k2)


def _peer_expert_up(xn_ref, u_ref, at_ref, piece):
    ts = slice(piece * PEER_UPIECE, (piece + 1) * PEER_UPIECE)
    r = lax.dot_general(u_ref[...], xn_ref[ts, :], NT_DIMS, preferred_element_type=F32)
    at_ref[:, ts] = r
    return r[r.shape[0] - SUBLANES:, PEER_UPIECE - LANES:]


def _after(row, anchor, zero_ref):
    z = pltpu.bitcast(anchor, jnp.int32) & zero_ref[...]
    return row + pltpu.bitcast(z, F32)[0:1, :]


def _peer_expert_gate(c, tb, hg_cur, e1_ref, e2_ref, thr_ref, at_ref, anchor, zero_ref):
    rs = slice(c * PEER_NKEYS, (c + 1) * PEER_NKEYS)
    ts = slice(tb * LANES, (tb + 1) * LANES)
    g = jnp.zeros((PEER_NKEYS, LANES), F32)
    for h in range(PEER_HEADS):
        e1_row = e1_ref[0, h, tb, c:c + 1, :]
        if anchor is not None:
            e1_row = _after(e1_row, anchor, zero_ref)
        w = e1_row * e2_ref[h, :, ts]
        g = g + jnp.where(w >= thr_ref[h, :, ts], w, 0.0)
    a = at_ref[rs, ts]
    gelu = 0.5 * a * (1.0 + lax.erf(a * (2.0 ** -0.5)))
    hg_cur[ts, rs] = (gelu * g).T.astype(BF16)


def _peer_expert_down(hg_prev, v_ref, o_ref, piece):
    cs = slice(piece * PEER_DPIECE, (piece + 1) * PEER_DPIECE)
    r = jnp.dot(hg_prev[...], v_ref[:, cs], preferred_element_type=F32)
    o_ref[:, cs] += r
    return r[r.shape[0] - SUBLANES:, PEER_DPIECE - LANES:]


def _peer_dense_kernel(xn_ref, u_ref, v_ref, e1_ref, e2_ref, thr_ref, h_hbm, gf_ref, zero_ref, o_ref,
                       at_ref, hg0_ref, hg1_ref, h_ref, h_sem, *, te, n_tiles):
    h_copy = pltpu.make_async_copy(
        h_hbm.at[pl.ds(pl.multiple_of(pl.program_id(0) * h_ref.shape[0], h_ref.shape[0]), h_ref.shape[0]), :],
        h_ref, h_sem)
    e = pl.program_id(1)
    n_sub = te // PEER_NKEYS
    tm = xn_ref.shape[0]
    n_up = tm // PEER_UPIECE
    n_down = o_ref.shape[1] // PEER_DPIECE
    hg = (hg0_ref, hg1_ref)
    tiles = [(c, tb) for c in range(n_sub) for tb in range(tm // LANES)]

    def body(par, up, down, first=False, last=False):
        if first:
            h_copy.start()
            o_ref[...] = jnp.zeros_like(o_ref)
        if last:
            h_copy.wait()
        todo = list(tiles) if up else []
        anchor = None
        if up:
            for k in range(n_up):
                _peer_expert_up(xn_ref, u_ref, at_ref, k)
        for k in range(n_down if down else 0):
            done = _peer_expert_down(hg[1 - par], v_ref, o_ref, k)
            left = max(0.0, 1.0 - (k + 1) / (PEER_GATE_SPAN * n_down))
            n_now = len(todo) - round(len(tiles) * left) if todo else 0
            for c, tb in todo[:n_now]:
                _peer_expert_gate(c, tb, hg[par], e1_ref, e2_ref, thr_ref, at_ref, anchor, zero_ref)
            todo = todo[n_now:]
            anchor = done
        for c, tb in todo:
            _peer_expert_gate(c, tb, hg[par], e1_ref, e2_ref, thr_ref, at_ref, None, zero_ref)
        if last:
            o_ref[...] = _rms_norm(h_ref[...] + o_ref[...], gf_ref[...])

    pl.when(e == 0)(functools.partial(body, 0, True, False, first=True))
    for par in (0, 1):
        pl.when((e >= 1) & (e < n_tiles) & (e % 2 == par))(functools.partial(body, par, True, True))
    pl.when(e == n_tiles)(functools.partial(body, n_tiles % 2, False, True, last=True))


def _peer_dense(xn2, u, v, e1, e2, thr, h1, gf, tm, te):
    rows, d = h1.shape
    n_tiles = u.shape[0] // te
    zero = jnp.zeros((SUBLANES, LANES), jnp.int32)
    return pl.pallas_call(
        functools.partial(_peer_dense_kernel, te=te, n_tiles=n_tiles),
        grid=(rows // tm, n_tiles + 1),
        in_specs=[
            pl.BlockSpec((tm, d), lambda i, e: (i, 0), pipeline_mode=pl.Buffered(1)),
            pl.BlockSpec((te, d), lambda i, e: (jnp.minimum(e, n_tiles - 1), 0)),
            pl.BlockSpec((te, d), lambda i, e: (jnp.maximum(e - 1, 0), 0)),
            pl.BlockSpec((1, PEER_HEADS, tm // LANES, te // PEER_NKEYS, LANES),
                         lambda i, e: (jnp.minimum(e, n_tiles - 1), 0, i, 0, 0)),
            pl.BlockSpec((PEER_HEADS, PEER_NKEYS, tm), lambda i, e: (0, 0, i), pipeline_mode=pl.Buffered(1)),
            pl.BlockSpec((PEER_HEADS, 1, tm), lambda i, e: (0, 0, i)),
            pl.BlockSpec(memory_space=pl.ANY),
            pl.BlockSpec((1, d), lambda i, e: (0, 0)),
            pl.BlockSpec((SUBLANES, LANES), lambda i, e: (0, 0)),
        ],
        out_specs=pl.BlockSpec((tm, d), lambda i, e: (i, 0), pipeline_mode=pl.Buffered(1)),
        out_shape=jax.ShapeDtypeStruct((rows, d), F32),
        scratch_shapes=[pltpu.VMEM((te, tm), F32), pltpu.VMEM((tm, te), BF16), pltpu.VMEM((tm, te), BF16),
                        pltpu.VMEM((tm, d), F32), pltpu.SemaphoreType.DMA(())],
        compiler_params=_params(("arbitrary", "arbitrary")),
        name="peer_dense",
    )(xn2, u, v, e1, e2, thr, h1, gf, zero)


def _tile(n, pref):
    t = min(n, pref)
    assert n % t == 0, (n, t)
    return t


def kernel(x, meta_tokens, norm1_g, w_in, w_gate_up, b_gate, gla_norm_g, w_dw, b_dw, conv_ln_g, conv_ln_b,
           w_out, norm2_g, peer_wq, peer_keys1, peer_keys2, peer_u, peer_v, final_norm_g):
    batch, seq, d = x.shape
    rows = batch * seq
    assert norm1_g.shape[0] == 1, "single-layer block"
    assert seq % CHUNK == 0 and meta_tokens.shape[0] <= CONV_HALO and (CHUNK - meta_tokens.shape[0]) >= 0
    x2 = x.reshape(rows, d)

    r_end = 2 * GLA_KEY_WIDTH + 2 * GLA_WIDTH
    gd_end = r_end + GLA_GATE_RANK
    w_in_t = w_in[0].T.astype(BF16)
    w_gd_t = jnp.pad(w_in_t[r_end:gd_end], ((0, LANES - GLA_GATE_RANK), (0, 0)))
    wgu = jnp.pad(w_gate_up[0], ((0, LANES - GLA_GATE_RANK), (0, 0))).astype(BF16)
    g1 = norm1_g[0][None]

    tm = _tile(rows, 512)
    te = 512
    rb, tr, hps = _tile(seq, 512), _tile(seq, 128), 2
    in_steps = (rows // tm) * ((w_in_t.shape[0] - GLA_GATE_RANK) // 1024)
    side_w = _cast_steps((w_out[0], peer_wq[0])) <= in_steps
    side_u = _cast_steps((peer_u[0],)) <= batch * (GLA_HEADS // hps) * (seq // rb)
    side_v = _cast_steps((peer_v[0],)) <= batch * (seq // tr)

    p, gd, *w_bf = _in_proj(x2, g1, w_in_t, w_gd_t, tm, 1024, r_end, GLA_GATE_RANK,
                            (w_out[0], peer_wq[0]) if side_w else ())
    w_out_bf, wq_bf = w_bf if side_w else (w_out[0].astype(BF16), peer_wq[0].astype(BF16))
    pm, gdm = _in_proj(meta_tokens.astype(F32), g1, w_in_t, w_gd_t, meta_tokens.shape[0], 1024,
                       r_end, GLA_GATE_RANK)

    gla_out, *u_bf = _gla(p, gd, pm, gdm, wgu, b_gate[0][None], gla_norm_g[0][None], batch, rb, hps,
                          (peer_u[0],) if side_u else ())
    u_bf = u_bf[0] if side_u else peer_u[0].astype(BF16)
    w_dw_p = jnp.pad(w_dw[0], ((0, CONV_HALO - CONV_KERNEL), (0, 0)))
    conv_out, *v_bf = _conv(p, pm, w_dw_p, b_dw[0][None], conv_ln_g[0][None], conv_ln_b[0][None], batch, tr,
                            (peer_v[0],) if side_v else ())
    v_bf = v_bf[0] if side_v else peer_v[0].astype(BF16)
    h1, = _out_proj(gla_out, conv_out, w_out_bf, x2, tm, _tile(d, 1024))

    xn2, e1, e2, thr = _peer_score(h1, norm2_g[0][None], wq_bf,
                                   peer_keys1[0].astype(BF16), peer_keys2[0].astype(BF16), tm, te // PEER_NKEYS)
    out = _peer_dense(xn2, u_bf, v_bf, e1, e2, thr, h1,
                      final_norm_g[None], tm, te)
    return out.reshape(batch, seq, d)
```
